```python
import jax, jax.numpy as jnp
from jax import lax
import numpy as np

D_MODEL = 2048
BATCH = 2
SEQ = 4096
DEPTH = 1

MIX_WIDTH = D_MODEL
A_WIDTH = MIX_WIDTH // 2
B_WIDTH = MIX_WIDTH - A_WIDTH
CHUNK = 128
A_HEAD_DIM = 128
A_HEADS = A_WIDTH // A_HEAD_DIM
POOL_WINDOWS = (2, 4, 8, 16)
B_GROUPS = len(POOL_WINDOWS)
B_GROUP_DIM = B_WIDTH // B_GROUPS
IN_WIDTH = 2 * A_WIDTH + B_WIDTH
D_FF = 4 * D_MODEL
EPS = 1e-6

kernel_name = "hybrid_sgu_pool_block"


def rmsnorm(x, g):
    xf = x.astype(jnp.float32)
    y = xf * lax.rsqrt(jnp.mean(xf * xf, axis=-1, keepdims=True) + EPS)
    return (y * g.astype(jnp.float32)).astype(x.dtype)


def spatial_gating(u, v, w_s, b_s, g_v):
    bsz, s, _ = u.shape
    n_chunks = s // CHUNK
    vh = rmsnorm(v.reshape(bsz, s, A_HEADS, A_HEAD_DIM), g_v.reshape(A_HEADS, A_HEAD_DIM))
    vh = vh.reshape(bsz, n_chunks, CHUNK, A_HEADS, A_HEAD_DIM)
    causal = jnp.tril(jnp.ones((CHUNK, CHUNK), dtype=bool))
    w = jnp.where(causal[None], w_s, jnp.zeros_like(w_s))
    mixed = jnp.einsum('hts,bcshd->bcthd', w, vh)
    mixed = mixed + jnp.transpose(b_s)[None, None, :, :, None]
    return u * mixed.reshape(bsz, s, A_WIDTH)


def multiscale_pool(z, w_pool, pool_scale):
    bsz, s, _ = z.shape
    zf = z.astype(jnp.float32).reshape(bsz, s, B_GROUPS, B_GROUP_DIM)
    csum = jnp.cumsum(zf, axis=1)
    cpad = jnp.concatenate([jnp.zeros_like(csum[:, :1]), csum], axis=1)
    pos = jnp.arange(s, dtype=jnp.int32)
    outs = []
    for g, win in enumerate(POOL_WINDOWS):
        c = cpad[:, :, g]
        lag = jnp.pad(c, ((0, 0), (win - 1, 0), (0, 0)))[:, :s]
        count = jnp.minimum(pos + 1, win).astype(jnp.float32)[None, :, None]
        outs.append((c[:, 1:] - lag) / count - zf[:, :, g])
    pooled = jnp.stack(outs, axis=2).astype(z.dtype)
    y = jnp.einsum('bsgc,gcd->bsgd', pooled, w_pool).reshape(bsz, s, B_WIDTH)
    return y * pool_scale


def setup_inputs(seed: int = 0) -> dict:
    key = jax.random.key(seed)
    ks = jax.random.split(key, 16)
    f32 = jnp.float32
    x = jax.random.normal(ks[0], (BATCH, SEQ, D_MODEL), f32)
    g_mix = 1.0 + 0.05 * jax.random.normal(ks[1], (DEPTH, D_MODEL), f32)
    w_in = jax.random.normal(ks[2], (DEPTH, D_MODEL, IN_WIDTH), f32) * D_MODEL ** -0.5
    g_v = 1.0 + 0.05 * jax.random.normal(ks[3], (DEPTH, A_WIDTH), f32)
    w_s = jax.random.normal(ks[4], (DEPTH, A_HEADS, CHUNK, CHUNK), f32) * (0.5 * CHUNK ** -0.5)
    b_s = 1.0 + 0.1 * jax.random.normal(ks[5], (DEPTH, A_HEADS, CHUNK), f32)
    w_pool = jax.random.normal(ks[6], (DEPTH, B_GROUPS, B_GROUP_DIM, B_GROUP_DIM), f32) * B_GROUP_DIM ** -0.5
    pool_scale = 0.5 + 0.1 * jax.random.normal(ks[7], (DEPTH, B_WIDTH), f32)
    w_out = jax.random.normal(ks[8], (DEPTH, MIX_WIDTH, D_MODEL), f32) * MIX_WIDTH ** -0.5
    g_ffn = 1.0 + 0.05 * jax.random.normal(ks[9], (DEPTH, D_MODEL), f32)
    w_up = jax.random.normal(ks[10], (DEPTH, D_MODEL, D_FF), f32) * D_MODEL ** -0.5
    w_down = jax.random.normal(ks[11], (DEPTH, D_FF, D_MODEL), f32) * D_FF ** -0.5
    g_final = 1.0 + 0.05 * jax.random.normal(ks[12], (D_MODEL,), f32)
    return {"x": x, "g_mix": g_mix, "w_in": w_in, "g_v": g_v, "w_s": w_s,
            "b_s": b_s, "w_pool": w_pool, "pool_scale": pool_scale,
            "w_out": w_out, "g_ffn": g_ffn, "w_up": w_up, "w_down": w_down,
            "g_final": g_final}


def reference(x, g_mix, w_in, g_v, w_s, b_s, w_pool, pool_scale, w_out,
              g_ffn, w_up, w_down, g_final):
    for layer in range(DEPTH):
        h = rmsnorm(x, g_mix[layer])
        proj = jnp.einsum('bsd,de->bse', h, w_in[layer])
        u = jax.nn.gelu(proj[..., :A_WIDTH])
        v = jax.nn.gelu(proj[..., A_WIDTH:2 * A_WIDTH])
        z = proj[..., 2 * A_WIDTH:]
        out_a = spatial_gating(u, v, w_s[layer], b_s[layer], g_v[layer])
        out_b = multiscale_pool(z, w_pool[layer], pool_scale[layer])
        mixed = jnp.concatenate([out_a, out_b], axis=-1)
        x = x + jnp.einsum('bse,ed->bsd', mixed, w_out[layer])
        h = rmsnorm(x, g_ffn[layer])
        act = jnp.square(jax.nn.relu(jnp.einsum('bsd,df->bsf', h, w_up[layer])))
        x = x + jnp.einsum('bsf,fd->bsd', act, w_down[layer])
    return rmsnorm(x, g_final)
```

```python
import functools

import jax
import jax.numpy as jnp
from jax import lax
from jax.experimental import pallas as pl
from jax.experimental.pallas import tpu as pltpu

CHUNK = 128
A_HEAD_DIM = 128
POOL_WINDOWS = (2, 4, 8, 16)
EPS = 1e-6

HALO = 16
MIX_TM = 512
FFN_TM = 512
FFN_TF = 1024
VMEM_LIMIT_BYTES = 60 * 1024 * 1024

_bf16 = jnp.bfloat16
_f32 = jnp.float32


def _rms_scale(x):
    return lax.rsqrt(jnp.mean(x * x, axis=-1, keepdims=True) + EPS)


def _dot(a, b):
    return jnp.dot(a, b, preferred_element_type=_f32)


def _mix_kernel(x_ref, gmix_ref, win_ref, gv_ref, ws_ref, bst_ref, wpool_ref,
                pscale_ref, wout_ref, o_ref, zext_ref, mixed_ref, *, seq_len):
    tm = x_ref.shape[0]
    a_width = gv_ref.shape[1]
    n_heads = a_width // A_HEAD_DIM
    group_dim = wpool_ref.shape[1]
    seq_pos0 = (pl.program_id(0) * tm) % seq_len

    x = x_ref[...]
    hb = (x * _rms_scale(x) * gmix_ref[...]).astype(_bf16)

    u = jax.nn.gelu(_dot(hb, win_ref[:, 0:a_width]))
    v = jax.nn.gelu(_dot(hb, win_ref[:, a_width:2 * a_width]))
    row = lax.broadcasted_iota(jnp.int32, (CHUNK, CHUNK), 0)
    col = lax.broadcasted_iota(jnp.int32, (CHUNK, CHUNK), 1)
    causal = row >= col
    for hd in range(n_heads):
        cs = slice(hd * A_HEAD_DIM, (hd + 1) * A_HEAD_DIM)
        vh = v[:, cs]
        vn = (vh * _rms_scale(vh) * gv_ref[:, cs]).astype(_bf16)
        w = jnp.where(causal, ws_ref[hd], 0.0).astype(_bf16)
        bias = bst_ref[:, hd:hd + 1]
        for c in range(tm // CHUNK):
            rs = slice(c * CHUNK, (c + 1) * CHUNK)
            mixed_ref[rs, cs] = (u[rs, cs] * (_dot(w, vn[rs]) + bias)).astype(_bf16)

    z = _dot(hb, win_ref[:, 2 * a_width:])

    @pl.when(seq_pos0 == 0)
    def _():
        zext_ref[0:HALO, :] = jnp.zeros((HALO, zext_ref.shape[1]), _f32)

    zext_ref[HALO:HALO + tm, :] = z
    pos = seq_pos0 + lax.broadcasted_iota(jnp.int32, (tm, 1), 0)
    for g, win in enumerate(POOL_WINDOWS):
        gs = slice(g * group_dim, (g + 1) * group_dim)
        zg = zext_ref[HALO:HALO + tm, gs]
        s = zg
        for k in range(1, win):
            s = s + zext_ref[HALO - k:HALO - k + tm, gs]
        count = jnp.minimum(pos + 1, win).astype(_f32)
        pooled = (s / count - zg).astype(_bf16)
        y = _dot(pooled, wpool_ref[g]) * pscale_ref[:, gs]
        mixed_ref[:, a_width + g * group_dim:a_width + (g + 1) * group_dim] = y.astype(_bf16)
    zext_ref[0:HALO, :] = zext_ref[tm:tm + HALO, :]

    o_ref[...] = x + _dot(mixed_ref[...], wout_ref[...])


def _ffn_kernel(x_ref, gffn_ref, wup_ref, wdown_ref, gfin_ref, o_ref, h_ref, acc_ref, *, final_norm):
    j = pl.program_id(1)

    @pl.when(j == 0)
    def _():
        x = x_ref[...]
        h_ref[...] = (x * _rms_scale(x) * gffn_ref[...]).astype(_bf16)
        acc_ref[...] = x

    act = jnp.square(jnp.maximum(_dot(h_ref[...], wup_ref[...]), 0.0)).astype(_bf16)
    acc_ref[...] += _dot(act, wdown_ref[...])

    @pl.when(j == pl.num_programs(1) - 1)
    def _():
        y = acc_ref[...]
        if final_norm:
            y = y * _rms_scale(y) * gfin_ref[...]
        o_ref[...] = y


def _resident(shape):
    zeros = (0,) * len(shape)
    return pl.BlockSpec(shape, lambda *_: zeros, pipeline_mode=pl.Buffered(1))


def _mix_layer(xf, g_mix, w_in, g_v, w_s, b_s, w_pool, pool_scale, w_out, *, seq_len):
    m, d = xf.shape
    a_width = g_v.shape[0]
    b_width = pool_scale.shape[0]
    tm = MIX_TM
    assert m % tm == 0 and seq_len % tm == 0 and tm % CHUNK == 0 and tm >= HALO
    tile = pl.BlockSpec((tm, d), lambda i: (i, 0))
    operands = (
        xf,
        g_mix.reshape(1, d),
        w_in.astype(_bf16),
        g_v.reshape(1, a_width),
        w_s,
        b_s.T,
        w_pool.astype(_bf16),
        pool_scale.reshape(1, b_width),
        w_out.astype(_bf16),
    )
    return pl.pallas_call(
        functools.partial(_mix_kernel, seq_len=seq_len),
        grid=(m // tm,),
        in_specs=[tile] + [_resident(op.shape) for op in operands[1:]],
        out_specs=tile,
        out_shape=jax.ShapeDtypeStruct((m, d), _f32),
        scratch_shapes=[
            pltpu.VMEM((HALO + tm, b_width), _f32),
            pltpu.VMEM((tm, a_width + b_width), _bf16),
        ],
        compiler_params=pltpu.CompilerParams(
            dimension_semantics=("arbitrary",),
            vmem_limit_bytes=VMEM_LIMIT_BYTES),
        name="mix_layer",
    )(*operands)


def _ffn_layer(xf, g_ffn, w_up, w_down, g_final, *, final_norm):
    m, d = xf.shape
    d_ff = w_up.shape[1]
    tm, tf = FFN_TM, FFN_TF
    assert m % tm == 0 and d_ff % tf == 0
    tile = pl.BlockSpec((tm, d), lambda i, j: (i, 0))
    return pl.pallas_call(
        functools.partial(_ffn_kernel, final_norm=final_norm),
        grid=(m // tm, d_ff // tf),
        in_specs=[
            tile,
            _resident((1, d)),
            pl.BlockSpec((d, tf), lambda i, j: (0, j)),
            pl.BlockSpec((tf, d), lambda i, j: (j, 0)),
            _resident((1, d)),
        ],
        out_specs=tile,
        out_shape=jax.ShapeDtypeStruct((m, d), _f32),
        scratch_shapes=[
            pltpu.VMEM((tm, d), _bf16),
            pltpu.VMEM((tm, d), _f32),
        ],
        compiler_params=pltpu.CompilerParams(
            dimension_semantics=("arbitrary", "arbitrary"),
            vmem_limit_bytes=VMEM_LIMIT_BYTES),
        name="ffn_layer",
    )(xf, g_ffn.reshape(1, d), w_up.astype(_bf16), w_down.astype(_bf16), g_final.reshape(1, d))


def kernel(x, g_mix, w_in, g_v, w_s, b_s, w_pool, pool_scale, w_out, g_ffn, w_up, w_down, g_final):
    bsz, seq_len, d = x.shape
    depth = g_mix.shape[0]
    xf = x.reshape(bsz * seq_len, d)
    for layer in range(depth):
        xf = _mix_layer(xf, g_mix[layer], w_in[layer], g_v[layer], w_s[layer], b_s[layer],
                        w_pool[layer], pool_scale[layer], w_out[layer], seq_len=seq_len)
        xf = _ffn_layer(xf, g_ffn[layer], w_up[layer], w_down[layer], g_final,
                        final_norm=(layer == depth - 1))
    return xf.reshape(bsz, seq_len, d)
```

```python
import functools

import jax
import jax.numpy as jnp
from jax import lax
from jax.experimental import pallas as pl
from jax.experimental.pallas import tpu as pltpu

CHUNK = 128
A_HEAD_DIM = 128
POOL_WINDOWS = (2, 4, 8, 16)
EPS = 1e-6

HALO = 16
MIX_TM = 512
FFN_TM = 512
FFN_TF = 1024
VMEM_LIMIT_BYTES = 60 * 1024 * 1024

_bf16 = jnp.bfloat16
_f32 = jnp.float32


def _rms_scale(x):
    return lax.rsqrt(jnp.mean(x * x, axis=-1, keepdims=True) + EPS)


def _dot(a, b):
    return jnp.dot(a, b, preferred_element_type=_f32)


def _mix_kernel(x_ref, gmix_ref, win_ref, gv_ref, ws_ref, bst_ref, wpool_ref,
                pscale_ref, wout_ref, o_ref, zext_ref, mixed_ref, *, seq_len):
    tm = x_ref.shape[0]
    a_width = gv_ref.shape[1]
    n_heads = a_width // A_HEAD_DIM
    group_dim = wpool_ref.shape[1]
    seq_pos0 = (pl.program_id(0) * tm) % seq_len
    next_seq_pos0 = ((pl.program_id(0) + 1) * tm) % seq_len

    @pl.when(pl.program_id(0) == 0)
    def _():
        zext_ref[0:HALO, :] = jnp.zeros((HALO, zext_ref.shape[1]), _f32)

    x = x_ref[...]
    hb = (x * _rms_scale(x) * gmix_ref[...]).astype(_bf16)

    zext_ref[HALO:HALO + tm, :] = _dot(hb, win_ref[:, 2 * a_width:])
    v = jax.nn.gelu(_dot(hb, win_ref[:, a_width:2 * a_width]))
    u = jax.nn.gelu(_dot(hb, win_ref[:, 0:a_width]))

    pos = seq_pos0 + lax.broadcasted_iota(jnp.int32, (tm, 1), 0)
    for g, win in enumerate(POOL_WINDOWS):
        gs = slice(g * group_dim, (g + 1) * group_dim)
        s = zext_ref[:, gs]
        shift = 1
        while shift < win:
            s = s + pltpu.roll(s, shift, axis=0)
            shift *= 2
        count = jnp.minimum(pos + 1, win).astype(_f32)
        pooled = (s[HALO:] / count - zext_ref[HALO:HALO + tm, gs]).astype(_bf16)
        y = _dot(pooled, wpool_ref[g]) * pscale_ref[:, gs]
        mixed_ref[:, a_width + g * group_dim:a_width + (g + 1) * group_dim] = y.astype(_bf16)
    zext_ref[0:HALO, :] = jnp.where(next_seq_pos0 == 0, 0.0, zext_ref[tm:tm + HALO, :])

    row = lax.broadcasted_iota(jnp.int32, (CHUNK, CHUNK), 0)
    col = lax.broadcasted_iota(jnp.int32, (CHUNK, CHUNK), 1)
    causal = row >= col
    for hd in range(n_heads):
        cs = slice(hd * A_HEAD_DIM, (hd + 1) * A_HEAD_DIM)
        vh = v[:, cs]
        vn = (vh * _rms_scale(vh) * gv_ref[:, cs]).astype(_bf16)
        w = jnp.where(causal, ws_ref[hd], 0.0).astype(_bf16)
        bias = bst_ref[:, hd:hd + 1]
        for c in range(tm // CHUNK):
            rs = slice(c * CHUNK, (c + 1) * CHUNK)
            mixed_ref[rs, cs] = (u[rs, cs] * (_dot(w, vn[rs]) + bias)).astype(_bf16)

    o_ref[...] = x + _dot(mixed_ref[...], wout_ref[...])


def _ffn_kernel(x_ref, gffn_ref, wup_ref, wdown_ref, gfin_ref, o_ref, h_ref, acc_ref, *, final_norm):
    j = pl.program_id(1)

    @pl.when(j == 0)
    def _():
        x = x_ref[...]
        h_ref[...] = (x * _rms_scale(x) * gffn_ref[...]).astype(_bf16)
        acc_ref[...] = x

    act = jnp.square(jnp.maximum(_dot(h_ref[...], wup_ref[...]), 0.0)).astype(_bf16)
    acc_ref[...] += _dot(act, wdown_ref[...])

    @pl.when(j == pl.num_programs(1) - 1)
    def _():
        y = acc_ref[...]
        if final_norm:
            y = y * _rms_scale(y) * gfin_ref[...]
        o_ref[...] = y


def _resident(shape):
    zeros = (0,) * len(shape)
    return pl.BlockSpec(shape, lambda *_: zeros, pipeline_mode=pl.Buffered(1))


def _mix_layer(xf, g_mix, w_in, g_v, w_s, b_s, w_pool, pool_scale, w_out, *, seq_len):
    m, d = xf.shape
    a_width = g_v.shape[0]
    b_width = pool_scale.shape[0]
    tm = MIX_TM
    assert m % tm == 0 and seq_len % tm == 0 and tm % CHUNK == 0 and tm >= HALO
    tile = pl.BlockSpec((tm, d), lambda i: (i, 0))
    operands = (
        xf,
        g_mix.reshape(1, d),
        w_in.astype(_bf16),
        g_v.reshape(1, a_width),
        w_s,
        b_s.T,
        w_pool.astype(_bf16),
        pool_scale.reshape(1, b_width),
        w_out.astype(_bf16),
    )
    return pl.pallas_call(
        functools.partial(_mix_kernel, seq_len=seq_len),
        grid=(m // tm,),
        in_specs=[tile] + [_resident(op.shape) for op in operands[1:]],
        out_specs=tile,
        out_shape=jax.ShapeDtypeStruct((m, d), _f32),
        scratch_shapes=[
            pltpu.VMEM((HALO + tm, b_width), _f32),
            pltpu.VMEM((tm, a_width + b_width), _bf16),
        ],
        compiler_params=pltpu.CompilerParams(
            dimension_semantics=("arbitrary",),
            vmem_limit_bytes=VMEM_LIMIT_BYTES),
        name="mix_layer",
    )(*operands)


def _ffn_layer(xf, g_ffn, w_up, w_down, g_final, *, final_norm):
    m, d = xf.shape
    d_ff = w_up.shape[1]
    tm, tf = FFN_TM, FFN_TF
    assert m % tm == 0 and d_ff % tf == 0
    tile = pl.BlockSpec((tm, d), lambda i, j: (i, 0))
    return pl.pallas_call(
        functools.partial(_ffn_kernel, final_norm=final_norm),
        grid=(m // tm, d_ff // tf),
        in_specs=[
            tile,
            _resident((1, d)),
            pl.BlockSpec((d, tf), lambda i, j: (0, j)),
            pl.BlockSpec((tf, d), lambda i, j: (j, 0)),
            _resident((1, d)),
        ],
        out_specs=tile,
        out_shape=jax.ShapeDtypeStruct((m, d), _f32),
        scratch_shapes=[
            pltpu.VMEM((tm, d), _bf16),
            pltpu.VMEM((tm, d), _f32),
        ],
        compiler_params=pltpu.CompilerParams(
            dimension_semantics=("arbitrary", "arbitrary"),
            vmem_limit_bytes=VMEM_LIMIT_BYTES),
        name="ffn_layer",
    )(xf, g_ffn.reshape(1, d), w_up.astype(_bf16), w_down.astype(_bf16), g_final.reshape(1, d))


def kernel(x, g_mix, w_in, g_v, w_s, b_s, w_pool, pool_scale, w_out, g_ffn, w_up, w_down, g_final):
    bsz, seq_len, d = x.shape
    depth = g_mix.shape[0]
    xf = x.reshape(bsz * seq_len, d)
    for layer in range(depth):
        xf = _mix_layer(xf, g_mix[layer], w_in[layer], g_v[layer], w_s[layer], b_s[layer],
                        w_pool[layer], pool_scale[layer], w_out[layer], seq_len=seq_len)
        xf = _ffn_layer(xf, g_ffn[layer], w_up[layer], w_down[layer], g_final,
                        final_norm=(layer == depth - 1))
    return xf.reshape(bsz, seq_len, d)
```

```python
import functools

import jax
import jax.numpy as jnp
from jax import lax
from jax.experimental import pallas as pl
from jax.experimental.pallas import tpu as pltpu

CHUNK = 128
A_HEAD_DIM = 128
POOL_WINDOWS = (2, 4, 8, 16)
EPS = 1e-6

HALO = 16
POOL_ROWS = 64
BF16_SUBLANES = 16
MIX_TM = 256
FFN_TM = 512
FFN_TF = 1024
VMEM_LIMIT_BYTES = 60 * 1024 * 1024

_bf16 = jnp.bfloat16
_f32 = jnp.float32


def _rms_scale(x):
    return lax.rsqrt(jnp.mean(x * x, axis=-1, keepdims=True) + EPS)


def _dot(a, b):
    return jnp.dot(a, b, preferred_element_type=_f32)


def _mix_kernel(x_ref, gmix_ref, win_ref, gv_ref, ws_ref, bst_ref, wpool_ref, pscale_ref, wout_ref,
                wup_ref, wdown_ref, o_ref, wup_bf_ref, wdown_bf_ref,
                zext_ref, pooled_ref, mixed_ref, *, seq_len):
    tm = x_ref.shape[0]
    a_width = gv_ref.shape[1]
    n_heads = a_width // A_HEAD_DIM
    group_dim = wpool_ref.shape[1]
    step = pl.program_id(0)
    seq_pos0 = (step * tm) % seq_len
    next_seq_pos0 = ((step + 1) * tm) % seq_len

    @pl.when(step == 0)
    def _():
        zext_ref[0:HALO, :] = jnp.zeros((HALO, zext_ref.shape[1]), _f32)

    wup_bf_ref[...] = wup_ref[...].astype(_bf16)
    wdown_bf_ref[...] = wdown_ref[...].astype(_bf16)

    x = x_ref[...]
    hb = (x * _rms_scale(x) * gmix_ref[...]).astype(_bf16)

    def in_proj(c0, c1):
        return _dot(hb, win_ref[:, c0:c1])

    def pool_group(g):
        win = POOL_WINDOWS[g]
        gs = slice(g * group_dim, (g + 1) * group_dim)
        for r0 in range(0, tm, POOL_ROWS):
            zb = zext_ref[r0:r0 + HALO + POOL_ROWS, gs]
            s = zb
            shift = 1
            while shift < win:
                s = s + pltpu.roll(s, shift, axis=0)
                shift *= 2
            if r0 + 1 >= win:
                mean = s[HALO:] * (1.0 / win)
            else:
                pos = seq_pos0 + r0 + lax.broadcasted_iota(jnp.int32, (POOL_ROWS, 1), 0)
                mean = s[HALO:] / jnp.minimum(pos + 1, win).astype(_f32)
            pooled_ref[r0:r0 + POOL_ROWS, gs] = (mean - zb[HALO:]).astype(_bf16)

    def pool_project(g):
        gs = slice(g * group_dim, (g + 1) * group_dim)
        y = _dot(pooled_ref[:, gs], wpool_ref[g]) * pscale_ref[:, gs]
        mixed_ref[:, a_width + g * group_dim:a_width + (g + 1) * group_dim] = y.astype(_bf16)

    row = lax.broadcasted_iota(jnp.int32, (CHUNK, CHUNK), 0)
    col = lax.broadcasted_iota(jnp.int32, (CHUNK, CHUNK), 1)
    causal = row >= col

    def normed_heads(vpart, hd0):
        out = []
        for k in range(vpart.shape[1] // A_HEAD_DIM):
            vh = vpart[:, k * A_HEAD_DIM:(k + 1) * A_HEAD_DIM]
            cs = slice((hd0 + k) * A_HEAD_DIM, (hd0 + k + 1) * A_HEAD_DIM)
            out.append((vh * _rms_scale(vh) * gv_ref[:, cs]).astype(_bf16))
        return out

    def gate_heads(upart, vns, hd0):
        for k, vn in enumerate(vns):
            hd = hd0 + k
            w = jnp.where(causal, ws_ref[hd], 0.0).astype(_bf16)
            bias = bst_ref[:, hd:hd + 1]
            for c in range(tm // CHUNK):
                rs = slice(c * CHUNK, (c + 1) * CHUNK)
                uh = upart[rs, k * A_HEAD_DIM:(k + 1) * A_HEAD_DIM]
                mixed_ref[rs, hd * A_HEAD_DIM:(hd + 1) * A_HEAD_DIM] = (
                    uh * (_dot(w, vn[rs]) + bias)).astype(_bf16)

    u_col, v_col, z_col = 0, a_width, 2 * a_width
    half_a, half_b, half_h, half_g = a_width // 2, zext_ref.shape[1] // 2, n_heads // 2, len(POOL_WINDOWS) // 2
    zext_ref[HALO:HALO + tm, 0:half_b] = in_proj(z_col, z_col + half_b)
    zext_ref[HALO:HALO + tm, half_b:] = in_proj(z_col + half_b, z_col + 2 * half_b)
    for g in range(half_g):
        pool_group(g)
    for g in range(half_g):
        pool_project(g)
    v0 = jax.nn.gelu(in_proj(v_col, v_col + half_a))
    for g in range(half_g, 2 * half_g):
        pool_group(g)
    zext_ref[0:HALO, :] = jnp.where(next_seq_pos0 == 0, 0.0, zext_ref[tm:tm + HALO, :])
    for g in range(half_g, 2 * half_g):
        pool_project(g)
    v1 = jax.nn.gelu(in_proj(v_col + half_a, v_col + 2 * half_a))
    vn0 = normed_heads(v0, 0)
    u0 = jax.nn.gelu(in_proj(u_col, u_col + half_a))
    vn1 = normed_heads(v1, half_h)
    gate_heads(u0, vn0, 0)
    u1 = jax.nn.gelu(in_proj(u_col + half_a, u_col + 2 * half_a))
    gate_heads(u1, vn1, half_h)

    o_ref[...] = x + _dot(mixed_ref[...], wout_ref[...])


def _ffn_kernel(x_ref, gffn_ref, wup_ref, wdown_ref, gfin_ref, o_ref, h_ref, acc_ref, *, final_norm):
    j = pl.program_id(1)

    @pl.when(j == 0)
    def _():
        x = x_ref[...]
        h_ref[...] = (x * _rms_scale(x) * gffn_ref[...]).astype(_bf16)
        acc_ref[...] = x

    act = jnp.square(jnp.maximum(_dot(h_ref[...], wup_ref[...]), 0.0)).astype(_bf16)
    acc_ref[...] += _dot(act, wdown_ref[...])

    @pl.when(j == pl.num_programs(1) - 1)
    def _():
        y = acc_ref[...]
        if final_norm:
            y = y * _rms_scale(y) * gfin_ref[...]
        o_ref[...] = y


def _resident(shape):
    zeros = (0,) * len(shape)
    return pl.BlockSpec(shape, lambda *_: zeros, pipeline_mode=pl.Buffered(1))


def _row_slab(w, n_steps):
    rows = w.shape[0] // n_steps
    assert w.shape[0] % n_steps == 0 and rows % BF16_SUBLANES == 0
    return pl.BlockSpec((rows, w.shape[1]), lambda i: (i, 0))


def _mix_layer(xf, g_mix, w_in, g_v, w_s, b_s, w_pool, pool_scale, w_out, w_up, w_down, *, seq_len):
    m, d = xf.shape
    a_width = g_v.shape[0]
    b_width = pool_scale.shape[0]
    tm = MIX_TM
    n_steps = m // tm
    assert m % tm == 0 and seq_len % tm == 0 and tm % CHUNK == 0 and tm % POOL_ROWS == 0 and tm >= HALO
    tile = pl.BlockSpec((tm, d), lambda i: (i, 0))
    operands = (
        xf,
        g_mix.reshape(1, d),
        w_in.astype(_bf16),
        g_v.reshape(1, a_width),
        w_s,
        b_s.T,
        w_pool.astype(_bf16),
        pool_scale.reshape(1, b_width),
        w_out.astype(_bf16),
    )
    slabs = [_row_slab(w_up, n_steps), _row_slab(w_down, n_steps)]
    return pl.pallas_call(
        functools.partial(_mix_kernel, seq_len=seq_len),
        grid=(n_steps,),
        in_specs=[tile] + [_resident(op.shape) for op in operands[1:]] + slabs,
        out_specs=[tile] + slabs,
        out_shape=[jax.ShapeDtypeStruct((m, d), _f32),
                   jax.ShapeDtypeStruct(w_up.shape, _bf16),
                   jax.ShapeDtypeStruct(w_down.shape, _bf16)],
        scratch_shapes=[
            pltpu.VMEM((HALO + tm, b_width), _f32),
            pltpu.VMEM((tm, b_width), _bf16),
            pltpu.VMEM((tm, a_width + b_width), _bf16),
        ],
        compiler_params=pltpu.CompilerParams(
            dimension_semantics=("arbitrary",),
            vmem_limit_bytes=VMEM_LIMIT_BYTES),
        name="mix_layer",
    )(*operands, w_up, w_down)


def _ffn_layer(xf, g_ffn, w_up_bf, w_down_bf, g_final, *, final_norm):
    m, d = xf.shape
    d_ff = w_up_bf.shape[1]
    tm, tf = FFN_TM, FFN_TF
    assert m % tm == 0 and d_ff % tf == 0
    tile = pl.BlockSpec((tm, d), lambda i, j: (i, 0))
    return pl.pallas_call(
        functools.partial(_ffn_kernel, final_norm=final_norm),
        grid=(m // tm, d_ff // tf),
        in_specs=[
            tile,
            _resident((1, d)),
            pl.BlockSpec((d, tf), lambda i, j: (0, j)),
            pl.BlockSpec((tf, d), lambda i, j: (j, 0)),
            _resident((1, d)),
        ],
        out_specs=tile,
        out_shape=jax.ShapeDtypeStruct((m, d), _f32),
        scratch_shapes=[
            pltpu.VMEM((tm, d), _bf16),
            pltpu.VMEM((tm, d), _f32),
        ],
        compiler_params=pltpu.CompilerParams(
            dimension_semantics=("arbitrary", "arbitrary"),
            vmem_limit_bytes=VMEM_LIMIT_BYTES),
        name="ffn_layer",
    )(xf, g_ffn.reshape(1, d), w_up_bf, w_down_bf, g_final.reshape(1, d))


def kernel(x, g_mix, w_in, g_v, w_s, b_s, w_pool, pool_scale, w_out, g_ffn, w_up, w_down, g_final):
    bsz, seq_len, d = x.shape
    depth = g_mix.shape[0]
    xf = x.reshape(bsz * seq_len, d)
    for layer in range(depth):
        xf, w_up_bf, w_down_bf = _mix_layer(
            xf, g_mix[layer], w_in[layer], g_v[layer], w_s[layer], b_s[layer], w_pool[layer],
            pool_scale[layer], w_out[layer], w_up[layer], w_down[layer], seq_len=seq_len)
        xf = _ffn_layer(xf, g_ffn[layer], w_up_bf, w_down_bf, g_final, final_norm=(layer == depth - 1))
    return xf.reshape(bsz, seq_len, d)
```

```python
import functools

import jax
import jax.numpy as jnp
from jax import lax
from jax.experimental import pallas as pl
from jax.experimental.pallas import tpu as pltpu

CHUNK = 128
A_HEAD_DIM = 128
POOL_WINDOWS = (2, 4, 8, 16)
EPS = 1e-6

HALO = 16
POOL_ROWS = 64
BF16_SUBLANES = 16
MIX_TM = 256
FFN_TM = 512
FFN_TF = 2048
FFN_SUB = 512
VMEM_LIMIT_BYTES = 60 * 1024 * 1024

_bf16 = jnp.bfloat16
_f32 = jnp.float32


def _rms_scale(x):
    return lax.rsqrt(jnp.mean(x * x, axis=-1, keepdims=True) + EPS)


def _dot(a, b):
    return jnp.dot(a, b, preferred_element_type=_f32)


def _mix_kernel(x_ref, gmix_ref, win_ref, gv_ref, ws_ref, bst_ref, wpool_ref, pscale_ref, wout_ref,
                wup_ref, wdown_ref, o_ref, wup_bf_ref, wdown_bf_ref,
                zext_ref, pooled_ref, mixed_ref, *, seq_len):
    tm = x_ref.shape[0]
    a_width = gv_ref.shape[1]
    n_heads = a_width // A_HEAD_DIM
    group_dim = wpool_ref.shape[1]
    step = pl.program_id(0)
    seq_pos0 = (step * tm) % seq_len
    next_seq_pos0 = ((step + 1) * tm) % seq_len

    @pl.when(step == 0)
    def _():
        zext_ref[0:HALO, :] = jnp.zeros((HALO, zext_ref.shape[1]), _f32)

    wup_bf_ref[...] = wup_ref[...].astype(_bf16)
    wdown_bf_ref[...] = wdown_ref[...].astype(_bf16)

    x = x_ref[...]
    hb = (x * _rms_scale(x) * gmix_ref[...]).astype(_bf16)

    def in_proj(c0, c1):
        return _dot(hb, win_ref[:, c0:c1])

    def pool_group(g):
        win = POOL_WINDOWS[g]
        gs = slice(g * group_dim, (g + 1) * group_dim)
        for r0 in range(0, tm, POOL_ROWS):
            zb = zext_ref[r0:r0 + HALO + POOL_ROWS, gs]
            s = zb
            shift = 1
            while shift < win:
                s = s + pltpu.roll(s, shift, axis=0)
                shift *= 2
            if r0 + 1 >= win:
                mean = s[HALO:] * (1.0 / win)
            else:
                pos = seq_pos0 + r0 + lax.broadcasted_iota(jnp.int32, (POOL_ROWS, 1), 0)
                mean = s[HALO:] / jnp.minimum(pos + 1, win).astype(_f32)
            pooled_ref[r0:r0 + POOL_ROWS, gs] = (mean - zb[HALO:]).astype(_bf16)

    def pool_project(g):
        gs = slice(g * group_dim, (g + 1) * group_dim)
        y = _dot(pooled_ref[:, gs], wpool_ref[g]) * pscale_ref[:, gs]
        mixed_ref[:, a_width + g * group_dim:a_width + (g + 1) * group_dim] = y.astype(_bf16)

    row = lax.broadcasted_iota(jnp.int32, (CHUNK, CHUNK), 0)
    col = lax.broadcasted_iota(jnp.int32, (CHUNK, CHUNK), 1)
    causal = row >= col

    def normed_heads(vpart, hd0):
        out = []
        for k in range(vpart.shape[1] // A_HEAD_DIM):
            vh = vpart[:, k * A_HEAD_DIM:(k + 1) * A_HEAD_DIM]
            cs = slice((hd0 + k) * A_HEAD_DIM, (hd0 + k + 1) * A_HEAD_DIM)
            out.append((vh * _rms_scale(vh) * gv_ref[:, cs]).astype(_bf16))
        return out

    def gate_heads(upart, vns, hd0):
        for k, vn in enumerate(vns):
            hd = hd0 + k
            w = jnp.where(causal, ws_ref[hd], 0.0).astype(_bf16)
            bias = bst_ref[:, hd:hd + 1]
            for c in range(tm // CHUNK):
                rs = slice(c * CHUNK, (c + 1) * CHUNK)
                uh = upart[rs, k * A_HEAD_DIM:(k + 1) * A_HEAD_DIM]
                mixed_ref[rs, hd * A_HEAD_DIM:(hd + 1) * A_HEAD_DIM] = (
                    uh * (_dot(w, vn[rs]) + bias)).astype(_bf16)

    u_col, v_col, z_col = 0, a_width, 2 * a_width
    half_a, half_b, half_h, half_g = a_width // 2, zext_ref.shape[1] // 2, n_heads // 2, len(POOL_WINDOWS) // 2
    zext_ref[HALO:HALO + tm, 0:half_b] = in_proj(z_col, z_col + half_b)
    zext_ref[HALO:HALO + tm, half_b:] = in_proj(z_col + half_b, z_col + 2 * half_b)
    for g in range(half_g):
        pool_group(g)
    for g in range(half_g):
        pool_project(g)
    v0 = jax.nn.gelu(in_proj(v_col, v_col + half_a))
    for g in range(half_g, 2 * half_g):
        pool_group(g)
    zext_ref[0:HALO, :] = jnp.where(next_seq_pos0 == 0, 0.0, zext_ref[tm:tm + HALO, :])
    for g in range(half_g, 2 * half_g):
        pool_project(g)
    v1 = jax.nn.gelu(in_proj(v_col + half_a, v_col + 2 * half_a))
    vn0 = normed_heads(v0, 0)
    u0 = jax.nn.gelu(in_proj(u_col, u_col + half_a))
    vn1 = normed_heads(v1, half_h)
    gate_heads(u0, vn0, 0)
    u1 = jax.nn.gelu(in_proj(u_col + half_a, u_col + 2 * half_a))
    gate_heads(u1, vn1, half_h)

    o_ref[...] = x + _dot(mixed_ref[...], wout_ref[...])


def _ffn_kernel(x_ref, gffn_ref, wup_ref, wdown_ref, gfin_ref, o_ref, h_ref, *, final_norm):
    j = pl.program_id(1)
    tf = wup_ref.shape[1]

    @pl.when(j == 0)
    def _():
        x = x_ref[...]
        h_ref[...] = (x * _rms_scale(x) * gffn_ref[...]).astype(_bf16)
        o_ref[...] = x

    def up(k):
        cols = slice(k * FFN_SUB, (k + 1) * FFN_SUB)
        return jnp.square(jnp.maximum(_dot(h_ref[...], wup_ref[:, cols]), 0.0)).astype(_bf16)

    def down(k, act):
        o_ref[...] += _dot(act, wdown_ref[k * FFN_SUB:(k + 1) * FFN_SUB, :])

    n_sub = tf // FFN_SUB
    act = up(0)
    for k in range(1, n_sub):
        nxt = up(k)
        down(k - 1, act)
        act = nxt
    down(n_sub - 1, act)

    if final_norm:
        @pl.when(j == pl.num_programs(1) - 1)
        def _():
            y = o_ref[...]
            o_ref[...] = y * _rms_scale(y) * gfin_ref[...]


def _resident(shape):
    zeros = (0,) * len(shape)
    return pl.BlockSpec(shape, lambda *_: zeros, pipeline_mode=pl.Buffered(1))


def _row_slab(w, n_steps):
    rows = w.shape[0] // n_steps
    assert w.shape[0] % n_steps == 0 and rows % BF16_SUBLANES == 0
    return pl.BlockSpec((rows, w.shape[1]), lambda i: (i, 0))


def _mix_layer(xf, g_mix, w_in, g_v, w_s, b_s, w_pool, pool_scale, w_out, w_up, w_down, *, seq_len):
    m, d = xf.shape
    a_width = g_v.shape[0]
    b_width = pool_scale.shape[0]
    tm = MIX_TM
    n_steps = m // tm
    assert m % tm == 0 and seq_len % tm == 0 and tm % CHUNK == 0 and tm % POOL_ROWS == 0 and tm >= HALO
    tile = pl.BlockSpec((tm, d), lambda i: (i, 0))
    operands = (
        xf,
        g_mix.reshape(1, d),
        w_in.astype(_bf16),
        g_v.reshape(1, a_width),
        w_s,
        b_s.T,
        w_pool.astype(_bf16),
        pool_scale.reshape(1, b_width),
        w_out.astype(_bf16),
    )
    slabs = [_row_slab(w_up, n_steps), _row_slab(w_down, n_steps)]
    return pl.pallas_call(
        functools.partial(_mix_kernel, seq_len=seq_len),
        grid=(n_steps,),
        in_specs=[tile] + [_resident(op.shape) for op in operands[1:]] + slabs,
        out_specs=[tile] + slabs,
        out_shape=[jax.ShapeDtypeStruct((m, d), _f32),
                   jax.ShapeDtypeStruct(w_up.shape, _bf16),
                   jax.ShapeDtypeStruct(w_down.shape, _bf16)],
        scratch_shapes=[
            pltpu.VMEM((HALO + tm, b_width), _f32),
            pltpu.VMEM((tm, b_width), _bf16),
            pltpu.VMEM((tm, a_width + b_width), _bf16),
        ],
        compiler_params=pltpu.CompilerParams(
            dimension_semantics=("arbitrary",),
            vmem_limit_bytes=VMEM_LIMIT_BYTES),
        name="mix_layer",
    )(*operands, w_up, w_down)


def _ffn_layer(xf, g_ffn, w_up_bf, w_down_bf, g_final, *, final_norm):
    m, d = xf.shape
    d_ff = w_up_bf.shape[1]
    tm, tf = FFN_TM, FFN_TF
    assert m % tm == 0 and d_ff % tf == 0 and tf % FFN_SUB == 0
    tile = pl.BlockSpec((tm, d), lambda i, j: (i, 0))
    return pl.pallas_call(
        functools.partial(_ffn_kernel, final_norm=final_norm),
        grid=(m // tm, d_ff // tf),
        in_specs=[
            tile,
            _resident((1, d)),
            pl.BlockSpec((d, tf), lambda i, j: (0, j)),
            pl.BlockSpec((tf, d), lambda i, j: (j, 0)),
            _resident((1, d)),
        ],
        out_specs=tile,
        out_shape=jax.ShapeDtypeStruct((m, d), _f32),
        scratch_shapes=[
            pltpu.VMEM((tm, d), _bf16),
        ],
        compiler_params=pltpu.CompilerParams(
            dimension_semantics=("arbitrary", "arbitrary"),
            vmem_limit_bytes=VMEM_LIMIT_BYTES),
        name="ffn_layer",
    )(xf, g_ffn.reshape(1, d), w_up_bf, w_down_bf, g_final.reshape(1, d))


def kernel(x, g_mix, w_in, g_v, w_s, b_s, w_pool, pool_scale, w_out, g_ffn, w_up, w_down, g_final):
    bsz, seq_len, d = x.shape
    depth = g_mix.shape[0]
    xf = x.reshape(bsz * seq_len, d)
    for layer in range(depth):
        xf, w_up_bf, w_down_bf = _mix_layer(
            xf, g_mix[layer], w_in[layer], g_v[layer], w_s[layer], b_s[layer], w_pool[layer],
            pool_scale[layer], w_out[layer], w_up[layer], w_down[layer], seq_len=seq_len)
        xf = _ffn_layer(xf, g_ffn[layer], w_up_bf, w_down_bf, g_final, final_norm=(layer == depth - 1))
    return xf.reshape(bsz, seq_len, d)
```

```python
import functools

import jax
import jax.numpy as jnp
from jax import lax
from jax.experimental import pallas as pl
from jax.experimental.pallas import tpu as pltpu

CHUNK = 128
A_HEAD_DIM = 128
POOL_WINDOWS = (2, 4, 8, 16)
EPS = 1e-6

HALO = 16
POOL_ROWS = 64
BF16_SUBLANES = 16
MIX_TM = 256
FFN_TM = 1024
FFN_TF = 1024
FFN_SUB = 512
VMEM_LIMIT_BYTES = 60 * 1024 * 1024

_bf16 = jnp.bfloat16
_f32 = jnp.float32


def _rms_scale(x):
    return lax.rsqrt(jnp.mean(x * x, axis=-1, keepdims=True) + EPS)


def _dot(a, b):
    return jnp.dot(a, b, preferred_element_type=_f32)


def _mix_kernel(x_ref, gmix_ref, win_ref, gv_ref, ws_ref, bst_ref, wpool_ref, pscale_ref, wout_ref,
                wup_ref, wdown_ref, o_ref, wup_bf_ref, wdown_bf_ref,
                zext_ref, pooled_ref, mixed_ref, *, seq_len):
    tm = x_ref.shape[0]
    a_width = gv_ref.shape[1]
    n_heads = a_width // A_HEAD_DIM
    group_dim = wpool_ref.shape[1]
    step = pl.program_id(0)
    seq_pos0 = (step * tm) % seq_len
    next_seq_pos0 = ((step + 1) * tm) % seq_len

    @pl.when(step == 0)
    def _():
        zext_ref[0:HALO, :] = jnp.zeros((HALO, zext_ref.shape[1]), _f32)

    wup_bf_ref[...] = wup_ref[...].astype(_bf16)
    wdown_bf_ref[...] = wdown_ref[...].astype(_bf16)

    x = x_ref[...]
    hb = (x * _rms_scale(x) * gmix_ref[...]).astype(_bf16)

    def in_proj(c0, c1):
        return _dot(hb, win_ref[:, c0:c1])

    def pool_group(g):
        win = POOL_WINDOWS[g]
        gs = slice(g * group_dim, (g + 1) * group_dim)
        for r0 in range(0, tm, POOL_ROWS):
            zb = zext_ref[r0:r0 + HALO + POOL_ROWS, gs]
            s = zb
            shift = 1
            while shift < win:
                s = s + pltpu.roll(s, shift, axis=0)
                shift *= 2
            if r0 + 1 >= win:
                mean = s[HALO:] * (1.0 / win)
            else:
                pos = seq_pos0 + r0 + lax.broadcasted_iota(jnp.int32, (POOL_ROWS, 1), 0)
                mean = s[HALO:] / jnp.minimum(pos + 1, win).astype(_f32)
            pooled_ref[r0:r0 + POOL_ROWS, gs] = (mean - zb[HALO:]).astype(_bf16)

    def pool_project(g):
        gs = slice(g * group_dim, (g + 1) * group_dim)
        y = _dot(pooled_ref[:, gs], wpool_ref[g]) * pscale_ref[:, gs]
        mixed_ref[:, a_width + g * group_dim:a_width + (g + 1) * group_dim] = y.astype(_bf16)

    row = lax.broadcasted_iota(jnp.int32, (CHUNK, CHUNK), 0)
    col = lax.broadcasted_iota(jnp.int32, (CHUNK, CHUNK), 1)
    causal = row >= col

    def normed_heads(vpart, hd0):
        out = []
        for k in range(vpart.shape[1] // A_HEAD_DIM):
            vh = vpart[:, k * A_HEAD_DIM:(k + 1) * A_HEAD_DIM]
            cs = slice((hd0 + k) * A_HEAD_DIM, (hd0 + k + 1) * A_HEAD_DIM)
            out.append((vh * _rms_scale(vh) * gv_ref[:, cs]).astype(_bf16))
        return out

    def gate_heads(upart, vns, hd0):
        for k, vn in enumerate(vns):
            hd = hd0 + k
            w = jnp.where(causal, ws_ref[hd], 0.0).astype(_bf16)
            bias = bst_ref[:, hd:hd + 1]
            for c in range(tm // CHUNK):
                rs = slice(c * CHUNK, (c + 1) * CHUNK)
                uh = upart[rs, k * A_HEAD_DIM:(k + 1) * A_HEAD_DIM]
                mixed_ref[rs, hd * A_HEAD_DIM:(hd + 1) * A_HEAD_DIM] = (
                    uh * (_dot(w, vn[rs]) + bias)).astype(_bf16)

    u_col, v_col, z_col = 0, a_width, 2 * a_width
    half_a, half_b, half_h, half_g = a_width // 2, zext_ref.shape[1] // 2, n_heads // 2, len(POOL_WINDOWS) // 2
    zext_ref[HALO:HALO + tm, 0:half_b] = in_proj(z_col, z_col + half_b)
    zext_ref[HALO:HALO + tm, half_b:] = in_proj(z_col + half_b, z_col + 2 * half_b)
    for g in range(half_g):
        pool_group(g)
    for g in range(half_g):
        pool_project(g)
    v0 = jax.nn.gelu(in_proj(v_col, v_col + half_a))
    for g in range(half_g, 2 * half_g):
        pool_group(g)
    zext_ref[0:HALO, :] = jnp.where(next_seq_pos0 == 0, 0.0, zext_ref[tm:tm + HALO, :])
    for g in range(half_g, 2 * half_g):
        pool_project(g)
    v1 = jax.nn.gelu(in_proj(v_col + half_a, v_col + 2 * half_a))
    vn0 = normed_heads(v0, 0)
    u0 = jax.nn.gelu(in_proj(u_col, u_col + half_a))
    vn1 = normed_heads(v1, half_h)
    gate_heads(u0, vn0, 0)
    u1 = jax.nn.gelu(in_proj(u_col + half_a, u_col + 2 * half_a))
    gate_heads(u1, vn1, half_h)

    o_ref[...] = x + _dot(mixed_ref[...], wout_ref[...])


def _ffn_kernel(x_ref, gffn_ref, wup_ref, wdown_ref, gfin_ref, o_ref, h_ref, *, final_norm):
    j = pl.program_id(1)
    tf = wup_ref.shape[1]

    @pl.when(j == 0)
    def _():
        x = x_ref[...]
        h_ref[...] = (x * _rms_scale(x) * gffn_ref[...]).astype(_bf16)
        o_ref[...] = x

    def up(k):
        cols = slice(k * FFN_SUB, (k + 1) * FFN_SUB)
        return jnp.square(jnp.maximum(_dot(h_ref[...], wup_ref[:, cols]), 0.0)).astype(_bf16)

    def down(k, act):
        o_ref[...] += _dot(act, wdown_ref[k * FFN_SUB:(k + 1) * FFN_SUB, :])

    n_sub = tf // FFN_SUB
    act = up(0)
    for k in range(1, n_sub):
        nxt = up(k)
        down(k - 1, act)
        act = nxt
    down(n_sub - 1, act)

    if final_norm:
        @pl.when(j == pl.num_programs(1) - 1)
        def _():
            y = o_ref[...]
            o_ref[...] = y * _rms_scale(y) * gfin_ref[...]


def _resident(shape):
    zeros = (0,) * len(shape)
    return pl.BlockSpec(shape, lambda *_: zeros, pipeline_mode=pl.Buffered(1))


def _row_slab(w, n_steps):
    rows = w.shape[0] // n_steps
    assert w.shape[0] % n_steps == 0 and rows % BF16_SUBLANES == 0
    return pl.BlockSpec((rows, w.shape[1]), lambda i: (i, 0))


def _mix_layer(xf, g_mix, w_in, g_v, w_s, b_s, w_pool, pool_scale, w_out, w_up, w_down, *, seq_len):
    m, d = xf.shape
    a_width = g_v.shape[0]
    b_width = pool_scale.shape[0]
    tm = MIX_TM
    n_steps = m // tm
    assert m % tm == 0 and seq_len % tm == 0 and tm % CHUNK == 0 and tm % POOL_ROWS == 0 and tm >= HALO
    tile = pl.BlockSpec((tm, d), lambda i: (i, 0))
    operands = (
        xf,
        g_mix.reshape(1, d),
        w_in.astype(_bf16),
        g_v.reshape(1, a_width),
        w_s,
        b_s.T,
        w_pool.astype(_bf16),
        pool_scale.reshape(1, b_width),
        w_out.astype(_bf16),
    )
    slabs = [_row_slab(w_up, n_steps), _row_slab(w_down, n_steps)]
    return pl.pallas_call(
        functools.partial(_mix_kernel, seq_len=seq_len),
        grid=(n_steps,),
        in_specs=[tile] + [_resident(op.shape) for op in operands[1:]] + slabs,
        out_specs=[tile] + slabs,
        out_shape=[jax.ShapeDtypeStruct((m, d), _f32),
                   jax.ShapeDtypeStruct(w_up.shape, _bf16),
                   jax.ShapeDtypeStruct(w_down.shape, _bf16)],
        scratch_shapes=[
            pltpu.VMEM((HALO + tm, b_width), _f32),
            pltpu.VMEM((tm, b_width), _bf16),
            pltpu.VMEM((tm, a_width + b_width), _bf16),
        ],
        compiler_params=pltpu.CompilerParams(
            dimension_semantics=("arbitrary",),
            vmem_limit_bytes=VMEM_LIMIT_BYTES),
        name="mix_layer",
    )(*operands, w_up, w_down)


def _ffn_layer(xf, g_ffn, w_up_bf, w_down_bf, g_final, *, final_norm):
    m, d = xf.shape
    d_ff = w_up_bf.shape[1]
    tm, tf = FFN_TM, FFN_TF
    assert m % tm == 0 and d_ff % tf == 0 and tf % FFN_SUB == 0
    tile = pl.BlockSpec((tm, d), lambda i, j: (i, 0))
    return pl.pallas_call(
        functools.partial(_ffn_kernel, final_norm=final_norm),
        grid=(m // tm, d_ff // tf),
        in_specs=[
            tile,
            _resident((1, d)),
            pl.BlockSpec((d, tf), lambda i, j: (0, j)),
            pl.BlockSpec((tf, d), lambda i, j: (j, 0)),
            _resident((1, d)),
        ],
        out_specs=tile,
        out_shape=jax.ShapeDtypeStruct((m, d), _f32),
        scratch_shapes=[
            pltpu.VMEM((tm, d), _bf16),
        ],
        compiler_params=pltpu.CompilerParams(
            dimension_semantics=("arbitrary", "arbitrary"),
            vmem_limit_bytes=VMEM_LIMIT_BYTES),
        name="ffn_layer",
    )(xf, g_ffn.reshape(1, d), w_up_bf, w_down_bf, g_final.reshape(1, d))


def kernel(x, g_mix, w_in, g_v, w_s, b_s, w_pool, pool_scale, w_out, g_ffn, w_up, w_down, g_final):
    bsz, seq_len, d = x.shape
    depth = g_mix.shape[0]
    xf = x.reshape(bsz * seq_len, d)
    for layer in range(depth):
        xf, w_up_bf, w_down_bf = _mix_layer(
            xf, g_mix[layer], w_in[layer], g_v[layer], w_s[layer], b_s[layer], w_pool[layer],
            pool_scale[layer], w_out[layer], w_up[layer], w_down[layer], seq_len=seq_len)
        xf = _ffn_layer(xf, g_ffn[layer], w_up_bf, w_down_bf, g_final, final_norm=(layer == depth - 1))
    return xf.reshape(bsz, seq_len, d)
```

```python
import functools

import jax
import jax.numpy as jnp
from jax import lax
from jax.experimental import pallas as pl
from jax.experimental.pallas import tpu as pltpu

CHUNK = 128
A_HEAD_DIM = 128
POOL_WINDOWS = (2, 4, 8, 16)
EPS = 1e-6

HALO = 16
POOL_ROWS = 64
BF16_SUBLANES = 16
MIX_TM = 256
FFN_TM = 512
FFN_TF = 2048
FFN_SUB = 512
VMEM_LIMIT_BYTES = 60 * 1024 * 1024

LOAD_ROWS = 256
LOAD_COLS = 1024
LOAD_SLOTS = 4

_bf16 = jnp.bfloat16
_f32 = jnp.float32


def _rms_scale(x):
    return lax.rsqrt(jnp.mean(x * x, axis=-1, keepdims=True) + EPS)


def _dot(a, b):
    return jnp.dot(a, b, preferred_element_type=_f32)


def _load_as_bf16(pairs, stage, sem):
    pieces = [(src, dst, r, c)
              for src, dst in pairs
              for c in range(0, src.shape[1], LOAD_COLS)
              for r in range(0, src.shape[0], LOAD_ROWS)]

    def read(i):
        src, _, r, c = pieces[i]
        slot = i % LOAD_SLOTS
        return pltpu.make_async_copy(src.at[r:r + LOAD_ROWS, c:c + LOAD_COLS], stage.at[slot], sem.at[slot])

    for i in range(min(LOAD_SLOTS - 1, len(pieces))):
        read(i).start()
    for i, (_, dst, r, c) in enumerate(pieces):
        if i + LOAD_SLOTS - 1 < len(pieces):
            read(i + LOAD_SLOTS - 1).start()
        read(i).wait()
        dst[r:r + LOAD_ROWS, c:c + LOAD_COLS] = stage[i % LOAD_SLOTS].astype(_bf16)


def _mix_kernel(x_ref, gmix_ref, gv_ref, ws_ref, bst_ref, pscale_ref,
                win_hbm, wpool_hbm, wout_hbm, wup_ref, wdown_ref,
                o_ref, wup_bf_ref, wdown_bf_ref,
                win_ref, wpool_ref, wout_ref, zext_ref, pooled_ref, mixed_ref,
                stage_ref, pool_stage_ref, load_sem, pool_sem, *, seq_len):
    tm = x_ref.shape[0]
    a_width = gv_ref.shape[1]
    n_heads = a_width // A_HEAD_DIM
    group_dim = wpool_ref.shape[1]
    step = pl.program_id(0)
    seq_pos0 = (step * tm) % seq_len
    next_seq_pos0 = ((step + 1) * tm) % seq_len

    @pl.when(step == 0)
    def _():
        zext_ref[0:HALO, :] = jnp.zeros((HALO, zext_ref.shape[1]), _f32)
        pool_copy = pltpu.make_async_copy(wpool_hbm, pool_stage_ref, pool_sem.at[0])
        pool_copy.start()
        _load_as_bf16([(win_hbm, win_ref), (wout_hbm, wout_ref)], stage_ref, load_sem)
        pool_copy.wait()
        wpool_ref[...] = pool_stage_ref[...].astype(_bf16)

    wup_bf_ref[...] = wup_ref[...].astype(_bf16)
    wdown_bf_ref[...] = wdown_ref[...].astype(_bf16)

    x = x_ref[...]
    hb = (x * _rms_scale(x) * gmix_ref[...]).astype(_bf16)

    def in_proj(c0, c1):
        return _dot(hb, win_ref[:, c0:c1])

    def pool_group(g):
        win = POOL_WINDOWS[g]
        gs = slice(g * group_dim, (g + 1) * group_dim)
        for r0 in range(0, tm, POOL_ROWS):
            zb = zext_ref[r0:r0 + HALO + POOL_ROWS, gs]
            s = zb
            shift = 1
            while shift < win:
                s = s + pltpu.roll(s, shift, axis=0)
                shift *= 2
            if r0 + 1 >= win:
                mean = s[HALO:] * (1.0 / win)
            else:
                pos = seq_pos0 + r0 + lax.broadcasted_iota(jnp.int32, (POOL_ROWS, 1), 0)
                mean = s[HALO:] / jnp.minimum(pos + 1, win).astype(_f32)
            pooled_ref[r0:r0 + POOL_ROWS, gs] = (mean - zb[HALO:]).astype(_bf16)

    def pool_project(g):
        gs = slice(g * group_dim, (g + 1) * group_dim)
        y = _dot(pooled_ref[:, gs], wpool_ref[g]) * pscale_ref[:, gs]
        mixed_ref[:, a_width + g * group_dim:a_width + (g + 1) * group_dim] = y.astype(_bf16)

    row = lax.broadcasted_iota(jnp.int32, (CHUNK, CHUNK), 0)
    col = lax.broadcasted_iota(jnp.int32, (CHUNK, CHUNK), 1)
    causal = row >= col

    def normed_heads(vpart, hd0):
        out = []
        for k in range(vpart.shape[1] // A_HEAD_DIM):
            vh = vpart[:, k * A_HEAD_DIM:(k + 1) * A_HEAD_DIM]
            cs = slice((hd0 + k) * A_HEAD_DIM, (hd0 + k + 1) * A_HEAD_DIM)
            out.append((vh * _rms_scale(vh) * gv_ref[:, cs]).astype(_bf16))
        return out

    def gate_heads(upart, vns, hd0):
        for k, vn in enumerate(vns):
            hd = hd0 + k
            w = jnp.where(causal, ws_ref[hd], 0.0).astype(_bf16)
            bias = bst_ref[:, hd:hd + 1]
            for c in range(tm // CHUNK):
                rs = slice(c * CHUNK, (c + 1) * CHUNK)
                uh = upart[rs, k * A_HEAD_DIM:(k + 1) * A_HEAD_DIM]
                mixed_ref[rs, hd * A_HEAD_DIM:(hd + 1) * A_HEAD_DIM] = (
                    uh * (_dot(w, vn[rs]) + bias)).astype(_bf16)

    u_col, v_col, z_col = 0, a_width, 2 * a_width
    half_a, half_b, half_h, half_g = a_width // 2, zext_ref.shape[1] // 2, n_heads // 2, len(POOL_WINDOWS) // 2
    zext_ref[HALO:HALO + tm, 0:half_b] = in_proj(z_col, z_col + half_b)
    zext_ref[HALO:HALO + tm, half_b:] = in_proj(z_col + half_b, z_col + 2 * half_b)
    for g in range(half_g):
        pool_group(g)
    for g in range(half_g):
        pool_project(g)
    v0 = jax.nn.gelu(in_proj(v_col, v_col + half_a))
    for g in range(half_g, 2 * half_g):
        pool_group(g)
    zext_ref[0:HALO, :] = jnp.where(next_seq_pos0 == 0, 0.0, zext_ref[tm:tm + HALO, :])
    for g in range(half_g, 2 * half_g):
        pool_project(g)
    v1 = jax.nn.gelu(in_proj(v_col + half_a, v_col + 2 * half_a))
    vn0 = normed_heads(v0, 0)
    u0 = jax.nn.gelu(in_proj(u_col, u_col + half_a))
    vn1 = normed_heads(v1, half_h)
    gate_heads(u0, vn0, 0)
    u1 = jax.nn.gelu(in_proj(u_col + half_a, u_col + 2 * half_a))
    gate_heads(u1, vn1, half_h)

    o_ref[...] = x + _dot(mixed_ref[...], wout_ref[...])


def _ffn_kernel(x_ref, gffn_ref, wup_ref, wdown_ref, gfin_ref, o_ref, h_ref, *, final_norm):
    j = pl.program_id(1)
    tf = wup_ref.shape[1]

    @pl.when(j == 0)
    def _():
        x = x_ref[...]
        h_ref[...] = (x * _rms_scale(x) * gffn_ref[...]).astype(_bf16)
        o_ref[...] = x

    def up(k):
        cols = slice(k * FFN_SUB, (k + 1) * FFN_SUB)
        return jnp.square(jnp.maximum(_dot(h_ref[...], wup_ref[:, cols]), 0.0)).astype(_bf16)

    def down(k, act):
        o_ref[...] += _dot(act, wdown_ref[k * FFN_SUB:(k + 1) * FFN_SUB, :])

    n_sub = tf // FFN_SUB
    act = up(0)
    for k in range(1, n_sub):
        nxt = up(k)
        down(k - 1, act)
        act = nxt
    down(n_sub - 1, act)

    if final_norm:
        @pl.when(j == pl.num_programs(1) - 1)
        def _():
            y = o_ref[...]
            o_ref[...] = y * _rms_scale(y) * gfin_ref[...]


def _resident(shape):
    zeros = (0,) * len(shape)
    return pl.BlockSpec(shape, lambda *_: zeros, pipeline_mode=pl.Buffered(1))


def _row_slab(w, n_steps):
    rows = w.shape[0] // n_steps
    assert w.shape[0] % n_steps == 0 and rows % BF16_SUBLANES == 0
    return pl.BlockSpec((rows, w.shape[1]), lambda i: (i, 0))


def _mix_layer(xf, g_mix, w_in, g_v, w_s, b_s, w_pool, pool_scale, w_out, w_up, w_down, *, seq_len):
    m, d = xf.shape
    a_width = g_v.shape[0]
    b_width = pool_scale.shape[0]
    tm = MIX_TM
    n_steps = m // tm
    assert m % tm == 0 and seq_len % tm == 0 and tm % CHUNK == 0 and tm % POOL_ROWS == 0 and tm >= HALO
    for w in (w_in, w_out):
        assert w.shape[0] % LOAD_ROWS == 0 and w.shape[1] % LOAD_COLS == 0
    tile = pl.BlockSpec((tm, d), lambda i: (i, 0))
    hbm = pl.BlockSpec(memory_space=pl.ANY)
    small = (
        g_mix.reshape(1, d),
        g_v.reshape(1, a_width),
        w_s,
        b_s.T,
        pool_scale.reshape(1, b_width),
    )
    slabs = [_row_slab(w_up, n_steps), _row_slab(w_down, n_steps)]
    return pl.pallas_call(
        functools.partial(_mix_kernel, seq_len=seq_len),
        grid=(n_steps,),
        in_specs=[tile] + [_resident(op.shape) for op in small] + [hbm, hbm, hbm] + slabs,
        out_specs=[tile] + slabs,
        out_shape=[jax.ShapeDtypeStruct((m, d), _f32),
                   jax.ShapeDtypeStruct(w_up.shape, _bf16),
                   jax.ShapeDtypeStruct(w_down.shape, _bf16)],
        scratch_shapes=[
            pltpu.VMEM(w_in.shape, _bf16),
            pltpu.VMEM(w_pool.shape, _bf16),
            pltpu.VMEM(w_out.shape, _bf16),
            pltpu.VMEM((HALO + tm, b_width), _f32),
            pltpu.VMEM((tm, b_width), _bf16),
            pltpu.VMEM((tm, a_width + b_width), _bf16),
            pltpu.VMEM((LOAD_SLOTS, LOAD_ROWS, LOAD_COLS), _f32),
            pltpu.VMEM(w_pool.shape, _f32),
            pltpu.SemaphoreType.DMA((LOAD_SLOTS,)),
            pltpu.SemaphoreType.DMA((1,)),
        ],
        compiler_params=pltpu.CompilerParams(
            dimension_semantics=("arbitrary",),
            vmem_limit_bytes=VMEM_LIMIT_BYTES),
        name="mix_layer",
    )(xf, *small, w_in, w_pool, w_out, w_up, w_down)


def _ffn_layer(xf, g_ffn, w_up_bf, w_down_bf, g_final, *, final_norm):
    m, d = xf.shape
    d_ff = w_up_bf.shape[1]
    tm, tf = FFN_TM, FFN_TF
    assert m % tm == 0 and d_ff % tf == 0 and tf % FFN_SUB == 0
    tile = pl.BlockSpec((tm, d), lambda i, j: (i, 0))
    return pl.pallas_call(
        functools.partial(_ffn_kernel, final_norm=final_norm),
        grid=(m // tm, d_ff // tf),
        in_specs=[
            tile,
            _resident((1, d)),
            pl.BlockSpec((d, tf), lambda i, j: (0, j)),
            pl.BlockSpec((tf, d), lambda i, j: (j, 0)),
            _resident((1, d)),
        ],
        out_specs=tile,
        out_shape=jax.ShapeDtypeStruct((m, d), _f32),
        scratch_shapes=[
            pltpu.VMEM((tm, d), _bf16),
        ],
        compiler_params=pltpu.CompilerParams(
            dimension_semantics=("arbitrary", "arbitrary"),
            vmem_limit_bytes=VMEM_LIMIT_BYTES),
        name="ffn_layer",
    )(xf, g_ffn.reshape(1, d), w_up_bf, w_down_bf, g_final.reshape(1, d))


def kernel(x, g_mix, w_in, g_v, w_s, b_s, w_pool, pool_scale, w_out, g_ffn, w_up, w_down, g_final):
    bsz, seq_len, d = x.shape
    depth = g_mix.shape[0]
    xf = x.reshape(bsz * seq_len, d)
    for layer in range(depth):
        xf, w_up_bf, w_down_bf = _mix_layer(
            xf, g_mix[layer], w_in[layer], g_v[layer], w_s[layer], b_s[layer], w_pool[layer],
            pool_scale[layer], w_out[layer], w_up[layer], w_down[layer], seq_len=seq_len)
        xf = _ffn_layer(xf, g_ffn[layer], w_up_bf, w_down_bf, g_final, final_norm=(layer == depth - 1))
    return xf.reshape(bsz, seq_len, d)
```

```python
import functools

import jax
import jax.numpy as jnp
from jax import lax
from jax.experimental import pallas as pl
from jax.experimental.pallas import tpu as pltpu

CHUNK = 128
A_HEAD_DIM = 128
POOL_WINDOWS = (2, 4, 8, 16)
EPS = 1e-6

HALO = 16
POOL_ROWS = 64
BF16_SUBLANES = 16
MIX_TM = 256
FFN_TM = 512
FFN_TF = 2048
FFN_SUB = 512
VMEM_LIMIT_BYTES = 60 * 1024 * 1024

LOAD_ROWS = 256
LOAD_COLS = 1024
LOAD_SLOTS = 4

_bf16 = jnp.bfloat16
_f32 = jnp.float32


def _rms_scale(x):
    return lax.rsqrt(jnp.mean(x * x, axis=-1, keepdims=True) + EPS)


def _dot(a, b):
    return jnp.dot(a, b, preferred_element_type=_f32)


def _load_as_bf16(pairs, stage, sem):
    pieces = [(src, dst, r, c)
              for src, dst in pairs
              for c in range(0, src.shape[1], LOAD_COLS)
              for r in range(0, src.shape[0], LOAD_ROWS)]

    def read(i):
        src, _, r, c = pieces[i]
        slot = i % LOAD_SLOTS
        return pltpu.make_async_copy(src.at[r:r + LOAD_ROWS, c:c + LOAD_COLS], stage.at[slot], sem.at[slot])

    for i in range(min(LOAD_SLOTS - 1, len(pieces))):
        read(i).start()
    for i, (_, dst, r, c) in enumerate(pieces):
        if i + LOAD_SLOTS - 1 < len(pieces):
            read(i + LOAD_SLOTS - 1).start()
        read(i).wait()
        dst[r:r + LOAD_ROWS, c:c + LOAD_COLS] = stage[i % LOAD_SLOTS].astype(_bf16)


def _mix_kernel(x_ref, gmix_ref, gv_ref, ws_ref, bst_ref, pscale_ref,
                win_hbm, wpool_hbm, wout_hbm, wup_ref, wdown_ref,
                o_ref, wup_bf_ref, wdown_bf_ref,
                win_ref, wpool_ref, wout_ref, zext_ref, pooled_ref, mixed_ref,
                stage_ref, pool_stage_ref, load_sem, pool_sem, *, seq_len):
    tm = x_ref.shape[0]
    a_width = gv_ref.shape[1]
    n_heads = a_width // A_HEAD_DIM
    group_dim = wpool_ref.shape[1]
    step = pl.program_id(0)
    seq_pos0 = (step * tm) % seq_len
    next_seq_pos0 = ((step + 1) * tm) % seq_len

    @pl.when(step == 0)
    def _():
        zext_ref[0:HALO, :] = jnp.zeros((HALO, zext_ref.shape[1]), _f32)
        pool_copy = pltpu.make_async_copy(wpool_hbm, pool_stage_ref, pool_sem.at[0])
        pool_copy.start()
        _load_as_bf16([(win_hbm, win_ref), (wout_hbm, wout_ref)], stage_ref, load_sem)
        pool_copy.wait()
        wpool_ref[...] = pool_stage_ref[...].astype(_bf16)

    wup_bf_ref[...] = wup_ref[...].astype(_bf16)
    wdown_bf_ref[...] = wdown_ref[...].astype(_bf16)

    x = x_ref[...]
    hb = (x * _rms_scale(x) * gmix_ref[...]).astype(_bf16)

    def in_proj(c0, c1):
        return _dot(hb, win_ref[:, c0:c1])

    def pool_group(g):
        win = POOL_WINDOWS[g]
        gs = slice(g * group_dim, (g + 1) * group_dim)
        for r0 in range(0, tm, POOL_ROWS):
            zb = zext_ref[r0:r0 + HALO + POOL_ROWS, gs]
            s = zb
            shift = 1
            while shift < win:
                s = s + pltpu.roll(s, shift, axis=0)
                shift *= 2
            if r0 + 1 >= win:
                mean = s[HALO:] * (1.0 / win)
            else:
                pos = seq_pos0 + r0 + lax.broadcasted_iota(jnp.int32, (POOL_ROWS, 1), 0)
                mean = s[HALO:] / jnp.minimum(pos + 1, win).astype(_f32)
            pooled_ref[r0:r0 + POOL_ROWS, gs] = (mean - zb[HALO:]).astype(_bf16)

    def pool_project(g):
        gs = slice(g * group_dim, (g + 1) * group_dim)
        y = _dot(pooled_ref[:, gs], wpool_ref[g]) * pscale_ref[:, gs]
        mixed_ref[:, a_width + g * group_dim:a_width + (g + 1) * group_dim] = y.astype(_bf16)

    row = lax.broadcasted_iota(jnp.int32, (CHUNK, CHUNK), 0)
    col = lax.broadcasted_iota(jnp.int32, (CHUNK, CHUNK), 1)
    causal = row >= col

    def normed_heads(vpart, hd0):
        out = []
        for k in range(vpart.shape[1] // A_HEAD_DIM):
            vh = vpart[:, k * A_HEAD_DIM:(k + 1) * A_HEAD_DIM]
            cs = slice((hd0 + k) * A_HEAD_DIM, (hd0 + k + 1) * A_HEAD_DIM)
            out.append((vh * _rms_scale(vh) * gv_ref[:, cs]).astype(_bf16))
        return out

    def gate_heads(upart, vns, hd0):
        for k, vn in enumerate(vns):
            hd = hd0 + k
            w = jnp.where(causal, ws_ref[hd], 0.0).astype(_bf16)
            bias = bst_ref[:, hd:hd + 1]
            for c in range(tm // CHUNK):
                rs = slice(c * CHUNK, (c + 1) * CHUNK)
                uh = upart[rs, k * A_HEAD_DIM:(k + 1) * A_HEAD_DIM]
                mixed_ref[rs, hd * A_HEAD_DIM:(hd + 1) * A_HEAD_DIM] = (
                    uh * (_dot(w, vn[rs]) + bias)).astype(_bf16)

    u_col, v_col, z_col = 0, a_width, 2 * a_width
    half_a, half_b, half_h, half_g = a_width // 2, zext_ref.shape[1] // 2, n_heads // 2, len(POOL_WINDOWS) // 2
    zext_ref[HALO:HALO + tm, 0:half_b] = in_proj(z_col, z_col + half_b)
    zext_ref[HALO:HALO + tm, half_b:] = in_proj(z_col + half_b, z_col + 2 * half_b)
    for g in range(half_g):
        pool_group(g)
    for g in range(half_g):
        pool_project(g)
    v0 = jax.nn.gelu(in_proj(v_col, v_col + half_a))
    for g in range(half_g, 2 * half_g):
        pool_group(g)
    zext_ref[0:HALO, :] = jnp.where(next_seq_pos0 == 0, 0.0, zext_ref[tm:tm + HALO, :])
    for g in range(half_g, 2 * half_g):
        pool_project(g)
    v1 = jax.nn.gelu(in_proj(v_col + half_a, v_col + 2 * half_a))
    vn0 = normed_heads(v0, 0)
    u0 = jax.nn.gelu(in_proj(u_col, u_col + half_a))
    vn1 = normed_heads(v1, half_h)
    gate_heads(u0, vn0, 0)
    u1 = jax.nn.gelu(in_proj(u_col + half_a, u_col + 2 * half_a))
    gate_heads(u1, vn1, half_h)

    o_ref[...] = x + _dot(mixed_ref[...], wout_ref[...])


def _ffn_kernel(x_ref, gffn_ref, wup_ref, wdown_ref, gfin_ref, o_ref, h_ref, *, final_norm):
    i, j = pl.program_id(0), pl.program_id(1)
    n_j = pl.num_programs(1)
    n_sub = wup_ref.shape[1] // FFN_SUB
    h_cur, h_next = h_ref.at[i % 2], h_ref.at[(i + 1) % 2]

    tm = x_ref.shape[0]

    def normalise_input(dst, r0=0, r1=tm):
        x = x_ref[r0:r1, :]
        dst[r0:r1, :] = (x * _rms_scale(x) * gffn_ref[...]).astype(_bf16)

    @pl.when((i == 0) & (j == 0))
    def _():
        normalise_input(h_cur)

    def step(first, last):
        def up(k):
            cols = slice(k * FFN_SUB, (k + 1) * FFN_SUB)
            return jnp.square(jnp.maximum(_dot(h_cur[...], wup_ref[:, cols]), 0.0)).astype(_bf16)

        def down(k, act):
            d = _dot(act, wdown_ref[k * FFN_SUB:(k + 1) * FFN_SUB, :])
            if first and k == 0:
                o_ref[...] = x_ref[...] + d
            elif last and k == n_sub - 1 and final_norm:
                y = o_ref[...] + d
                o_ref[...] = y * _rms_scale(y) * gfin_ref[...]
            else:
                o_ref[...] += d

        act = up(0)
        for k in range(1, n_sub):
            if last:
                rows = tm // (n_sub - 1) // BF16_SUBLANES * BF16_SUBLANES
                normalise_input(h_next, (k - 1) * rows, tm if k == n_sub - 1 else k * rows)
            nxt = up(k)
            down(k - 1, act)
            act = nxt
        down(n_sub - 1, act)

    pl.when(j == 0)(functools.partial(step, True, False))
    pl.when((j > 0) & (j < n_j - 1))(functools.partial(step, False, False))
    pl.when(j == n_j - 1)(functools.partial(step, False, True))


def _resident(shape):
    zeros = (0,) * len(shape)
    return pl.BlockSpec(shape, lambda *_: zeros, pipeline_mode=pl.Buffered(1))


def _row_slab(w, n_steps):
    rows = w.shape[0] // n_steps
    assert w.shape[0] % n_steps == 0 and rows % BF16_SUBLANES == 0
    return pl.BlockSpec((rows, w.shape[1]), lambda i: (i, 0))


def _mix_layer(xf, g_mix, w_in, g_v, w_s, b_s, w_pool, pool_scale, w_out, w_up, w_down, *, seq_len):
    m, d = xf.shape
    a_width = g_v.shape[0]
    b_width = pool_scale.shape[0]
    tm = MIX_TM
    n_steps = m // tm
    assert m % tm == 0 and seq_len % tm == 0 and tm % CHUNK == 0 and tm % POOL_ROWS == 0 and tm >= HALO
    for w in (w_in, w_out):
        assert w.shape[0] % LOAD_ROWS == 0 and w.shape[1] % LOAD_COLS == 0
    tile = pl.BlockSpec((tm, d), lambda i: (i, 0))
    hbm = pl.BlockSpec(memory_space=pl.ANY)
    small = (
        g_mix.reshape(1, d),
        g_v.reshape(1, a_width),
        w_s,
        b_s.T,
        pool_scale.reshape(1, b_width),
    )
    slabs = [_row_slab(w_up, n_steps), _row_slab(w_down, n_steps)]
    return pl.pallas_call(
        functools.partial(_mix_kernel, seq_len=seq_len),
        grid=(n_steps,),
        in_specs=[tile] + [_resident(op.shape) for op in small] + [hbm, hbm, hbm] + slabs,
        out_specs=[tile] + slabs,
        out_shape=[jax.ShapeDtypeStruct((m, d), _f32),
                   jax.ShapeDtypeStruct(w_up.shape, _bf16),
                   jax.ShapeDtypeStruct(w_down.shape, _bf16)],
        scratch_shapes=[
            pltpu.VMEM(w_in.shape, _bf16),
            pltpu.VMEM(w_pool.shape, _bf16),
            pltpu.VMEM(w_out.shape, _bf16),
            pltpu.VMEM((HALO + tm, b_width), _f32),
            pltpu.VMEM((tm, b_width), _bf16),
            pltpu.VMEM((tm, a_width + b_width), _bf16),
            pltpu.VMEM((LOAD_SLOTS, LOAD_ROWS, LOAD_COLS), _f32),
            pltpu.VMEM(w_pool.shape, _f32),
            pltpu.SemaphoreType.DMA((LOAD_SLOTS,)),
            pltpu.SemaphoreType.DMA((1,)),
        ],
        compiler_params=pltpu.CompilerParams(
            dimension_semantics=("arbitrary",),
            vmem_limit_bytes=VMEM_LIMIT_BYTES),
        name="mix_layer",
    )(xf, *small, w_in, w_pool, w_out, w_up, w_down)


def _ffn_layer(xf, g_ffn, w_up_bf, w_down_bf, g_final, *, final_norm):
    m, d = xf.shape
    d_ff = w_up_bf.shape[1]
    tm, tf = FFN_TM, FFN_TF
    n_tiles, n_j = m // tm, d_ff // tf
    assert m % tm == 0 and d_ff % tf == 0 and tf % FFN_SUB == 0 and n_j >= 2
    tile = pl.BlockSpec((tm, d), lambda i, j: (i, 0))
    tile_ahead = pl.BlockSpec((tm, d), lambda i, j: (jnp.minimum(i + (j + 1) // n_j, n_tiles - 1), 0))
    return pl.pallas_call(
        functools.partial(_ffn_kernel, final_norm=final_norm),
        grid=(n_tiles, n_j),
        in_specs=[
            tile_ahead,
            _resident((1, d)),
            pl.BlockSpec((d, tf), lambda i, j: (0, j)),
            pl.BlockSpec((tf, d), lambda i, j: (j, 0)),
            _resident((1, d)),
        ],
        out_specs=tile,
        out_shape=jax.ShapeDtypeStruct((m, d), _f32),
        scratch_shapes=[
            pltpu.VMEM((2, tm, d), _bf16),
        ],
        compiler_params=pltpu.CompilerParams(
            dimension_semantics=("arbitrary", "arbitrary"),
            vmem_limit_bytes=VMEM_LIMIT_BYTES),
        name="ffn_layer",
    )(xf, g_ffn.reshape(1, d), w_up_bf, w_down_bf, g_final.reshape(1, d))


def kernel(x, g_mix, w_in, g_v, w_s, b_s, w_pool, pool_scale, w_out, g_ffn, w_up, w_down, g_final):
    bsz, seq_len, d = x.shape
    depth = g_mix.shape[0]
    xf = x.reshape(bsz * seq_len, d)
    for layer in range(depth):
        xf, w_up_bf, w_down_bf = _mix_layer(
            xf, g_mix[layer], w_in[layer], g_v[layer], w_s[layer], b_s[layer], w_pool[layer],
            pool_scale[layer], w_out[layer], w_up[layer], w_down[layer], seq_len=seq_len)
        xf = _ffn_layer(xf, g_ffn[layer], w_up_bf, w_down_bf, g_final, final_norm=(layer == depth - 1))
    return xf.reshape(bsz, seq_len, d)
```

```python
import functools

import jax
import jax.numpy as jnp
from jax import lax
from jax.experimental import pallas as pl
from jax.experimental.pallas import tpu as pltpu

CHUNK = 128
A_HEAD_DIM = 128
POOL_WINDOWS = (2, 4, 8, 16)
EPS = 1e-6

HALO = 16
POOL_ROWS = 64
BF16_SUBLANES = 16
MIX_TM = 256
FFN_TM = 512
FFN_TF = 2048
FFN_SUB = 512
VMEM_LIMIT_BYTES = 60 * 1024 * 1024

LOAD_ROWS = 256
LOAD_COLS = 1024
LOAD_SLOTS = 4

_bf16 = jnp.bfloat16
_f32 = jnp.float32


def _rms_scale(x):
    return lax.rsqrt(jnp.mean(x * x, axis=-1, keepdims=True) + EPS)


def _gelu_tanh(x):
    c = 0.7978845608028654
    t = jnp.tanh(x * ((x * x) * (c * 0.044715) + c))
    return x * (0.5 * t + 0.5)


def _dot(a, b):
    return jnp.dot(a, b, preferred_element_type=_f32)


def _load_as_bf16(pairs, stage, sem):
    pieces = [(src, dst, r, c)
              for src, dst in pairs
              for c in range(0, src.shape[1], LOAD_COLS)
              for r in range(0, src.shape[0], LOAD_ROWS)]

    def read(i):
        src, _, r, c = pieces[i]
        slot = i % LOAD_SLOTS
        return pltpu.make_async_copy(src.at[r:r + LOAD_ROWS, c:c + LOAD_COLS], stage.at[slot], sem.at[slot])

    for i in range(min(LOAD_SLOTS - 1, len(pieces))):
        read(i).start()
    for i, (_, dst, r, c) in enumerate(pieces):
        if i + LOAD_SLOTS - 1 < len(pieces):
            read(i + LOAD_SLOTS - 1).start()
        read(i).wait()
        dst[r:r + LOAD_ROWS, c:c + LOAD_COLS] = stage[i % LOAD_SLOTS].astype(_bf16)


def _mix_kernel(x_ref, gmix_ref, gv_ref, ws_ref, bst_ref, pscale_ref,
                win_hbm, wpool_hbm, wout_hbm, wup_ref, wdown_ref,
                o_ref, wup_bf_ref, wdown_bf_ref,
                win_ref, wpool_ref, wout_ref, zext_ref, pooled_ref, mixed_ref,
                stage_ref, pool_stage_ref, load_sem, pool_sem, *, seq_len):
    tm = x_ref.shape[0]
    a_width = gv_ref.shape[1]
    n_heads = a_width // A_HEAD_DIM
    group_dim = wpool_ref.shape[1]
    step = pl.program_id(0)
    seq_pos0 = (step * tm) % seq_len
    next_seq_pos0 = ((step + 1) * tm) % seq_len

    @pl.when(step == 0)
    def _():
        zext_ref[0:HALO, :] = jnp.zeros((HALO, zext_ref.shape[1]), _f32)
        pool_copy = pltpu.make_async_copy(wpool_hbm, pool_stage_ref, pool_sem.at[0])
        pool_copy.start()
        _load_as_bf16([(win_hbm, win_ref), (wout_hbm, wout_ref)], stage_ref, load_sem)
        pool_copy.wait()
        wpool_ref[...] = pool_stage_ref[...].astype(_bf16)

    wup_bf_ref[...] = wup_ref[...].astype(_bf16)
    wdown_bf_ref[...] = wdown_ref[...].astype(_bf16)

    x = x_ref[...]
    hb = (x * _rms_scale(x) * gmix_ref[...]).astype(_bf16)

    def in_proj(c0, c1):
        return _dot(hb, win_ref[:, c0:c1])

    def pool_group(g):
        win = POOL_WINDOWS[g]
        gs = slice(g * group_dim, (g + 1) * group_dim)
        for r0 in range(0, tm, POOL_ROWS):
            zb = zext_ref[r0:r0 + HALO + POOL_ROWS, gs]
            s = zb
            shift = 1
            while shift < win:
                s = s + pltpu.roll(s, shift, axis=0)
                shift *= 2
            if r0 + 1 >= win:
                mean = s[HALO:] * (1.0 / win)
            else:
                pos = seq_pos0 + r0 + lax.broadcasted_iota(jnp.int32, (POOL_ROWS, 1), 0)
                mean = s[HALO:] / jnp.minimum(pos + 1, win).astype(_f32)
            pooled_ref[r0:r0 + POOL_ROWS, gs] = (mean - zb[HALO:]).astype(_bf16)

    def pool_project(g):
        gs = slice(g * group_dim, (g + 1) * group_dim)
        y = _dot(pooled_ref[:, gs], wpool_ref[g]) * pscale_ref[:, gs]
        mixed_ref[:, a_width + g * group_dim:a_width + (g + 1) * group_dim] = y.astype(_bf16)

    row = lax.broadcasted_iota(jnp.int32, (CHUNK, CHUNK), 0)
    col = lax.broadcasted_iota(jnp.int32, (CHUNK, CHUNK), 1)
    causal = row >= col

    def normed_heads(vpart, hd0):
        out = []
        for k in range(vpart.shape[1] // A_HEAD_DIM):
            vh = vpart[:, k * A_HEAD_DIM:(k + 1) * A_HEAD_DIM]
            cs = slice((hd0 + k) * A_HEAD_DIM, (hd0 + k + 1) * A_HEAD_DIM)
            out.append((vh * _rms_scale(vh) * gv_ref[:, cs]).astype(_bf16))
        return out

    def gate_heads(upart, vns, hd0):
        for k, vn in enumerate(vns):
            hd = hd0 + k
            w = jnp.where(causal, ws_ref[hd], 0.0).astype(_bf16)
            bias = bst_ref[:, hd:hd + 1]
            n_chunks = tm // CHUNK
            vn_wide = jnp.concatenate([vn[c * CHUNK:(c + 1) * CHUNK] for c in range(n_chunks)], axis=1)
            mixed_wide = _dot(w, vn_wide) + bias
            for c in range(n_chunks):
                rs = slice(c * CHUNK, (c + 1) * CHUNK)
                uh = upart[rs, k * A_HEAD_DIM:(k + 1) * A_HEAD_DIM]
                mixed_ref[rs, hd * A_HEAD_DIM:(hd + 1) * A_HEAD_DIM] = (
                    uh * mixed_wide[:, c * A_HEAD_DIM:(c + 1) * A_HEAD_DIM]).astype(_bf16)

    u_col, v_col, z_col = 0, a_width, 2 * a_width
    half_a, half_b, half_h, half_g = a_width // 2, zext_ref.shape[1] // 2, n_heads // 2, len(POOL_WINDOWS) // 2
    zext_ref[HALO:HALO + tm, 0:half_b] = in_proj(z_col, z_col + half_b)
    zext_ref[HALO:HALO + tm, half_b:] = in_proj(z_col + half_b, z_col + 2 * half_b)
    for g in range(half_g):
        pool_group(g)
    for g in range(half_g):
        pool_project(g)
    v0 = _gelu_tanh(in_proj(v_col, v_col + half_a))
    for g in range(half_g, 2 * half_g):
        pool_group(g)
    zext_ref[0:HALO, :] = jnp.where(next_seq_pos0 == 0, 0.0, zext_ref[tm:tm + HALO, :])
    for g in range(half_g, 2 * half_g):
        pool_project(g)
    v1 = _gelu_tanh(in_proj(v_col + half_a, v_col + 2 * half_a))
    vn0 = normed_heads(v0, 0)
    u0 = _gelu_tanh(in_proj(u_col, u_col + half_a))
    vn1 = normed_heads(v1, half_h)
    gate_heads(u0, vn0, 0)
    u1 = _gelu_tanh(in_proj(u_col + half_a, u_col + 2 * half_a))
    gate_heads(u1, vn1, half_h)

    o_ref[...] = x + _dot(mixed_ref[...], wout_ref[...])


def _ffn_kernel(x_ref, gffn_ref, wup_ref, wdown_ref, gfin_ref, o_ref, h_ref, *, final_norm):
    i, j = pl.program_id(0), pl.program_id(1)
    n_j = pl.num_programs(1)
    n_sub = wup_ref.shape[1] // FFN_SUB
    h_cur, h_next = h_ref.at[i % 2], h_ref.at[(i + 1) % 2]

    tm = x_ref.shape[0]

    def normalise_input(dst, r0=0, r1=tm):
        x = x_ref[r0:r1, :]
        dst[r0:r1, :] = (x * _rms_scale(x) * gffn_ref[...]).astype(_bf16)

    @pl.when((i == 0) & (j == 0))
    def _():
        normalise_input(h_cur)

    def step(first, last):
        def up(k):
            cols = slice(k * FFN_SUB, (k + 1) * FFN_SUB)
            r = jnp.maximum(_dot(h_cur[...], wup_ref[:, cols]).astype(_bf16), 0.0)
            return r * r

        def down(k, act):
            d = _dot(act, wdown_ref[k * FFN_SUB:(k + 1) * FFN_SUB, :])
            if first and k == 0:
                o_ref[...] = x_ref[...] + d
            elif last and k == n_sub - 1 and final_norm:
                y = o_ref[...] + d
                o_ref[...] = y * _rms_scale(y) * gfin_ref[...]
            else:
                o_ref[...] += d

        act = up(0)
        for k in range(1, n_sub):
            if last:
                rows = tm // (n_sub - 1) // BF16_SUBLANES * BF16_SUBLANES
                normalise_input(h_next, (k - 1) * rows, tm if k == n_sub - 1 else k * rows)
            nxt = up(k)
            down(k - 1, act)
            act = nxt
        down(n_sub - 1, act)

    pl.when(j == 0)(functools.partial(step, True, False))
    pl.when((j > 0) & (j < n_j - 1))(functools.partial(step, False, False))
    pl.when(j == n_j - 1)(functools.partial(step, False, True))


def _resident(shape):
    zeros = (0,) * len(shape)
    return pl.BlockSpec(shape, lambda *_: zeros, pipeline_mode=pl.Buffered(1))


def _row_slab(w, n_steps):
    rows = w.shape[0] // n_steps
    assert w.shape[0] % n_steps == 0 and rows % BF16_SUBLANES == 0
    return pl.BlockSpec((rows, w.shape[1]), lambda i: (i, 0))


def _mix_layer(xf, g_mix, w_in, g_v, w_s, b_s, w_pool, pool_scale, w_out, w_up, w_down, *, seq_len):
    m, d = xf.shape
    a_width = g_v.shape[0]
    b_width = pool_scale.shape[0]
    tm = MIX_TM
    n_steps = m // tm
    assert m % tm == 0 and seq_len % tm == 0 and tm % CHUNK == 0 and tm % POOL_ROWS == 0 and tm >= HALO
    for w in (w_in, w_out):
        assert w.shape[0] % LOAD_ROWS == 0 and w.shape[1] % LOAD_COLS == 0
    tile = pl.BlockSpec((tm, d), lambda i: (i, 0))
    hbm = pl.BlockSpec(memory_space=pl.ANY)
    small = (
        g_mix.reshape(1, d),
        g_v.reshape(1, a_width),
        w_s,
        b_s.T,
        pool_scale.reshape(1, b_width),
    )
    slabs = [_row_slab(w_up, n_steps), _row_slab(w_down, n_steps)]
    return pl.pallas_call(
        functools.partial(_mix_kernel, seq_len=seq_len),
        grid=(n_steps,),
        in_specs=[tile] + [_resident(op.shape) for op in small] + [hbm, hbm, hbm] + slabs,
        out_specs=[tile] + slabs,
        out_shape=[jax.ShapeDtypeStruct((m, d), _f32),
                   jax.ShapeDtypeStruct(w_up.shape, _bf16),
                   jax.ShapeDtypeStruct(w_down.shape, _bf16)],
        scratch_shapes=[
            pltpu.VMEM(w_in.shape, _bf16),
            pltpu.VMEM(w_pool.shape, _bf16),
            pltpu.VMEM(w_out.shape, _bf16),
            pltpu.VMEM((HALO + tm, b_width), _f32),
            pltpu.VMEM((tm, b_width), _bf16),
            pltpu.VMEM((tm, a_width + b_width), _bf16),
            pltpu.VMEM((LOAD_SLOTS, LOAD_ROWS, LOAD_COLS), _f32),
            pltpu.VMEM(w_pool.shape, _f32),
            pltpu.SemaphoreType.DMA((LOAD_SLOTS,)),
            pltpu.SemaphoreType.DMA((1,)),
        ],
        compiler_params=pltpu.CompilerParams(
            dimension_semantics=("arbitrary",),
            vmem_limit_bytes=VMEM_LIMIT_BYTES),
        name="mix_layer",
    )(xf, *small, w_in, w_pool, w_out, w_up, w_down)


def _ffn_layer(xf, g_ffn, w_up_bf, w_down_bf, g_final, *, final_norm):
    m, d = xf.shape
    d_ff = w_up_bf.shape[1]
    tm, tf = FFN_TM, FFN_TF
    n_tiles, n_j = m // tm, d_ff // tf
    assert m % tm == 0 and d_ff % tf == 0 and tf % FFN_SUB == 0 and n_j >= 2
    tile = pl.BlockSpec((tm, d), lambda i, j: (i, 0))
    tile_ahead = pl.BlockSpec((tm, d), lambda i, j: (jnp.minimum(i + (j + 1) // n_j, n_tiles - 1), 0))
    return pl.pallas_call(
        functools.partial(_ffn_kernel, final_norm=final_norm),
        grid=(n_tiles, n_j),
        in_specs=[
            tile_ahead,
            _resident((1, d)),
            pl.BlockSpec((d, tf), lambda i, j: (0, j)),
            pl.BlockSpec((tf, d), lambda i, j: (j, 0)),
            _resident((1, d)),
        ],
        out_specs=tile,
        out_shape=jax.ShapeDtypeStruct((m, d), _f32),
        scratch_shapes=[
            pltpu.VMEM((2, tm, d), _bf16),
        ],
        compiler_params=pltpu.CompilerParams(
            dimension_semantics=("arbitrary", "arbitrary"),
            vmem_limit_bytes=VMEM_LIMIT_BYTES),
        name="ffn_layer",
    )(xf, g_ffn.reshape(1, d), w_up_bf, w_down_bf, g_final.reshape(1, d))


def kernel(x, g_mix, w_in, g_v, w_s, b_s, w_pool, pool_scale, w_out, g_ffn, w_up, w_down, g_final):
    bsz, seq_len, d = x.shape
    depth = g_mix.shape[0]
    xf = x.reshape(bsz * seq_len, d)
    for layer in range(depth):
        xf, w_up_bf, w_down_bf = _mix_layer(
            xf, g_mix[layer], w_in[layer], g_v[layer], w_s[layer], b_s[layer], w_pool[layer],
            pool_scale[layer], w_out[layer], w_up[layer], w_down[layer], seq_len=seq_len)
        xf = _ffn_layer(xf, g_ffn[layer], w_up_bf, w_down_bf, g_final, final_norm=(layer == depth - 1))
    return xf.reshape(bsz, seq_len, d)
```

```python
import functools

import jax
import jax.numpy as jnp
from jax import lax
from jax.experimental import pallas as pl
from jax.experimental.pallas import tpu as pltpu

CHUNK = 128
A_HEAD_DIM = 128
POOL_WINDOWS = (2, 4, 8, 16)
EPS = 1e-6

HALO = 16
POOL_ROWS = 64
BF16_SUBLANES = 16
MIX_TM = 256
FFN_TM = 512
FFN_TF = 2048
FFN_SUB = 512
VMEM_LIMIT_BYTES = 60 * 1024 * 1024

LOAD_ROWS = 256
LOAD_COLS = 1024
LOAD_SLOTS = 8

_bf16 = jnp.bfloat16
_f32 = jnp.float32


def _rms_scale(x):
    return lax.rsqrt(jnp.mean(x * x, axis=-1, keepdims=True) + EPS)


def _gelu_tanh(x):
    c = 0.7978845608028654
    t = jnp.tanh(x * ((x * x) * (c * 0.044715) + c))
    return x * (0.5 * t + 0.5)


def _dot(a, b):
    return jnp.dot(a, b, preferred_element_type=_f32)


class _WeightLoader:
    def __init__(self, col_blocks, stage, sem):
        self.pieces = [(src, dst, r, c) for src, dst, c in col_blocks for r in range(0, src.shape[0], LOAD_ROWS)]
        self.block_end = []
        for src, _, _ in col_blocks:
            self.block_end.append((self.block_end[-1] if self.block_end else 0) + src.shape[0] // LOAD_ROWS)
        self.stage, self.sem = stage, sem
        self.done = 0

    def _read(self, i):
        src, _, r, c = self.pieces[i]
        slot = i % LOAD_SLOTS
        return pltpu.make_async_copy(src.at[r:r + LOAD_ROWS, c:c + LOAD_COLS], self.stage.at[slot], self.sem.at[slot])

    def start(self):
        for i in range(min(LOAD_SLOTS - 1, len(self.pieces))):
            self._read(i).start()

    def need(self, n_blocks):
        while self.done < self.block_end[n_blocks - 1]:
            i = self.done
            if i + LOAD_SLOTS - 1 < len(self.pieces):
                self._read(i + LOAD_SLOTS - 1).start()
            self._read(i).wait()
            _, dst, r, c = self.pieces[i]
            dst[r:r + LOAD_ROWS, c:c + LOAD_COLS] = self.stage[i % LOAD_SLOTS].astype(_bf16)
            self.done += 1


def _mix_kernel(x_ref, gmix_ref, gv_ref, ws_ref, bst_ref, pscale_ref,
                win_hbm, wpool_hbm, wout_hbm, wup_ref, wdown_ref,
                o_ref, wup_bf_ref, wdown_bf_ref,
                win_ref, wpool_ref, wout_ref, zext_ref, pooled_ref, mixed_ref,
                stage_ref, pool_stage_ref, load_sem, pool_sem, *, seq_len):
    tm = x_ref.shape[0]
    a_width = gv_ref.shape[1]
    n_heads = a_width // A_HEAD_DIM
    group_dim = wpool_ref.shape[1]
    b_width = zext_ref.shape[1]
    u_col, v_col, z_col = 0, a_width, 2 * a_width
    step = pl.program_id(0)
    seq_pos0 = (step * tm) % seq_len
    next_seq_pos0 = ((step + 1) * tm) % seq_len

    def tile_body(first_step):
        if first_step:
            zext_ref[0:HALO, :] = jnp.zeros((HALO, zext_ref.shape[1]), _f32)
            pool_copy = pltpu.make_async_copy(wpool_hbm, pool_stage_ref, pool_sem.at[0])
            pool_copy.start()
            in_blocks = [(win_hbm, win_ref, c)
                         for c0, width in ((z_col, b_width), (v_col, a_width), (u_col, a_width))
                         for c in range(c0, c0 + width, LOAD_COLS)]
            out_blocks = [(wout_hbm, wout_ref, c) for c in range(0, wout_ref.shape[1], LOAD_COLS)]
            loader = _WeightLoader(in_blocks + out_blocks, stage_ref, load_sem)
            loader.start()
        z_blocks, a_blocks = b_width // LOAD_COLS, a_width // LOAD_COLS

        def weights_ready(n_blocks):
            if first_step:
                loader.need(n_blocks)

        wup_bf_ref[...] = wup_ref[...].astype(_bf16)
        wdown_bf_ref[...] = wdown_ref[...].astype(_bf16)

        x = x_ref[...]
        hb = (x * _rms_scale(x) * gmix_ref[...]).astype(_bf16)

        def in_proj(c0, c1):
            return _dot(hb, win_ref[:, c0:c1])

        def pool_group(g):
            win = POOL_WINDOWS[g]
            gs = slice(g * group_dim, (g + 1) * group_dim)
            for r0 in range(0, tm, POOL_ROWS):
                zb = zext_ref[r0:r0 + HALO + POOL_ROWS, gs]
                s = zb
                shift = 1
                while shift < win:
                    s = s + pltpu.roll(s, shift, axis=0)
                    shift *= 2
                if r0 + 1 >= win:
                    mean = s[HALO:] * (1.0 / win)
                else:
                    pos = seq_pos0 + r0 + lax.broadcasted_iota(jnp.int32, (POOL_ROWS, 1), 0)
                    mean = s[HALO:] / jnp.minimum(pos + 1, win).astype(_f32)
                pooled_ref[r0:r0 + POOL_ROWS, gs] = (mean - zb[HALO:]).astype(_bf16)

        def pool_project(g):
            gs = slice(g * group_dim, (g + 1) * group_dim)
            y = _dot(pooled_ref[:, gs], wpool_ref[g]) * pscale_ref[:, gs]
            mixed_ref[:, a_width + g * group_dim:a_width + (g + 1) * group_dim] = y.astype(_bf16)

        row = lax.broadcasted_iota(jnp.int32, (CHUNK, CHUNK), 0)
        col = lax.broadcasted_iota(jnp.int32, (CHUNK, CHUNK), 1)
        causal = row >= col

        def normed_heads(vpart, hd0):
            out = []
            for k in range(vpart.shape[1] // A_HEAD_DIM):
                vh = vpart[:, k * A_HEAD_DIM:(k + 1) * A_HEAD_DIM]
                cs = slice((hd0 + k) * A_HEAD_DIM, (hd0 + k + 1) * A_HEAD_DIM)
                out.append((vh * _rms_scale(vh) * gv_ref[:, cs]).astype(_bf16))
            return out

        def gate_heads(upart, vns, hd0):
            for k, vn in enumerate(vns):
                hd = hd0 + k
                w = jnp.where(causal, ws_ref[hd], 0.0).astype(_bf16)
                bias = bst_ref[:, hd:hd + 1]
                n_chunks = tm // CHUNK
                vn_wide = jnp.concatenate([vn[c * CHUNK:(c + 1) * CHUNK] for c in range(n_chunks)], axis=1)
                mixed_wide = _dot(w, vn_wide) + bias
                for c in range(n_chunks):
                    rs = slice(c * CHUNK, (c + 1) * CHUNK)
                    uh = upart[rs, k * A_HEAD_DIM:(k + 1) * A_HEAD_DIM]
                    mixed_ref[rs, hd * A_HEAD_DIM:(hd + 1) * A_HEAD_DIM] = (
                        uh * mixed_wide[:, c * A_HEAD_DIM:(c + 1) * A_HEAD_DIM]).astype(_bf16)

        half_a, half_b, half_h, half_g = a_width // 2, zext_ref.shape[1] // 2, n_heads // 2, len(POOL_WINDOWS) // 2
        weights_ready(z_blocks)
        zext_ref[HALO:HALO + tm, 0:half_b] = in_proj(z_col, z_col + half_b)
        zext_ref[HALO:HALO + tm, half_b:] = in_proj(z_col + half_b, z_col + 2 * half_b)
        for g in range(half_g):
            pool_group(g)
        if first_step:
            pool_copy.wait()
            wpool_ref[...] = pool_stage_ref[...].astype(_bf16)
        for g in range(half_g):
            pool_project(g)
        weights_ready(z_blocks + a_blocks)
        v0 = _gelu_tanh(in_proj(v_col, v_col + half_a))
        for g in range(half_g, 2 * half_g):
            pool_group(g)
        zext_ref[0:HALO, :] = jnp.where(next_seq_pos0 == 0, 0.0, zext_ref[tm:tm + HALO, :])
        for g in range(half_g, 2 * half_g):
            pool_project(g)
        v1 = _gelu_tanh(in_proj(v_col + half_a, v_col + 2 * half_a))
        vn0 = normed_heads(v0, 0)
        weights_ready(z_blocks + 2 * a_blocks)
        u0 = _gelu_tanh(in_proj(u_col, u_col + half_a))
        vn1 = normed_heads(v1, half_h)
        gate_heads(u0, vn0, 0)
        u1 = _gelu_tanh(in_proj(u_col + half_a, u_col + 2 * half_a))
        gate_heads(u1, vn1, half_h)

        weights_ready(z_blocks + 2 * a_blocks + wout_ref.shape[1] // LOAD_COLS)
        o_ref[...] = x + _dot(mixed_ref[...], wout_ref[...])

    pl.when(step == 0)(functools.partial(tile_body, True))
    pl.when(step > 0)(functools.partial(tile_body, False))


def _ffn_kernel(x_ref, gffn_ref, wup_ref, wdown_ref, gfin_ref, o_ref, h_ref, *, final_norm):
    i, j = pl.program_id(0), pl.program_id(1)
    n_j = pl.num_programs(1)
    n_sub = wup_ref.shape[1] // FFN_SUB
    h_cur, h_next = h_ref.at[i % 2], h_ref.at[(i + 1) % 2]

    tm = x_ref.shape[0]

    def normalise_input(dst, r0=0, r1=tm):
        x = x_ref[r0:r1, :]
        dst[r0:r1, :] = (x * _rms_scale(x) * gffn_ref[...]).astype(_bf16)

    @pl.when((i == 0) & (j == 0))
    def _():
        normalise_input(h_cur)

    def step(first, last):
        def up(k):
            cols = slice(k * FFN_SUB, (k + 1) * FFN_SUB)
            return jnp.square(jnp.maximum(_dot(h_cur[...], wup_ref[:, cols]), 0.0)).astype(_bf16)

        def down(k, act):
            d = _dot(act, wdown_ref[k * FFN_SUB:(k + 1) * FFN_SUB, :])
            if first and k == 0:
                o_ref[...] = x_ref[...] + d
            elif last and k == n_sub - 1 and final_norm:
                y = o_ref[...] + d
                o_ref[...] = y * _rms_scale(y) * gfin_ref[...]
            else:
                o_ref[...] += d

        act = up(0)
        for k in range(1, n_sub):
            if last:
                rows = tm // (n_sub - 1) // BF16_SUBLANES * BF16_SUBLANES
                normalise_input(h_next, (k - 1) * rows, tm if k == n_sub - 1 else k * rows)
            nxt = up(k)
            down(k - 1, act)
            act = nxt
        down(n_sub - 1, act)

    pl.when(j == 0)(functools.partial(step, True, False))
    pl.when((j > 0) & (j < n_j - 1))(functools.partial(step, False, False))
    pl.when(j == n_j - 1)(functools.partial(step, False, True))


def _resident(shape):
    zeros = (0,) * len(shape)
    return pl.BlockSpec(shape, lambda *_: zeros, pipeline_mode=pl.Buffered(1))


def _row_slab(w, n_steps):
    rows = w.shape[0] // n_steps
    assert w.shape[0] % n_steps == 0 and rows % BF16_SUBLANES == 0
    return pl.BlockSpec((rows, w.shape[1]), lambda i: (i, 0))


def _mix_layer(xf, g_mix, w_in, g_v, w_s, b_s, w_pool, pool_scale, w_out, w_up, w_down, *, seq_len):
    m, d = xf.shape
    a_width = g_v.shape[0]
    b_width = pool_scale.shape[0]
    tm = MIX_TM
    n_steps = m // tm
    assert m % tm == 0 and seq_len % tm == 0 and tm % CHUNK == 0 and tm % POOL_ROWS == 0 and tm >= HALO
    for w in (w_in, w_out):
        assert w.shape[0] % LOAD_ROWS == 0 and w.shape[1] % LOAD_COLS == 0
    assert a_width % LOAD_COLS == 0 and b_width % LOAD_COLS == 0
    tile = pl.BlockSpec((tm, d), lambda i: (i, 0))
    hbm = pl.BlockSpec(memory_space=pl.ANY)
    small = (
        g_mix.reshape(1, d),
        g_v.reshape(1, a_width),
        w_s,
        b_s.T,
        pool_scale.reshape(1, b_width),
    )
    slabs = [_row_slab(w_up, n_steps), _row_slab(w_down, n_steps)]
    return pl.pallas_call(
        functools.partial(_mix_kernel, seq_len=seq_len),
        grid=(n_steps,),
        in_specs=[tile] + [_resident(op.shape) for op in small] + [hbm, hbm, hbm] + slabs,
        out_specs=[tile] + slabs,
        out_shape=[jax.ShapeDtypeStruct((m, d), _f32),
                   jax.ShapeDtypeStruct(w_up.shape, _bf16),
                   jax.ShapeDtypeStruct(w_down.shape, _bf16)],
        scratch_shapes=[
            pltpu.VMEM(w_in.shape, _bf16),
            pltpu.VMEM(w_pool.shape, _bf16),
            pltpu.VMEM(w_out.shape, _bf16),
            pltpu.VMEM((HALO + tm, b_width), _f32),
            pltpu.VMEM((tm, b_width), _bf16),
            pltpu.VMEM((tm, a_width + b_width), _bf16),
            pltpu.VMEM((LOAD_SLOTS, LOAD_ROWS, LOAD_COLS), _f32),
            pltpu.VMEM(w_pool.shape, _f32),
            pltpu.SemaphoreType.DMA((LOAD_SLOTS,)),
            pltpu.SemaphoreType.DMA((1,)),
        ],
        compiler_params=pltpu.CompilerParams(
            dimension_semantics=("arbitrary",),
            vmem_limit_bytes=VMEM_LIMIT_BYTES),
        name="mix_layer",
    )(xf, *small, w_in, w_pool, w_out, w_up, w_down)


def _ffn_layer(xf, g_ffn, w_up_bf, w_down_bf, g_final, *, final_norm):
    m, d = xf.shape
    d_ff = w_up_bf.shape[1]
    tm, tf = FFN_TM, FFN_TF
    n_tiles, n_j = m // tm, d_ff // tf
    assert m % tm == 0 and d_ff % tf == 0 and tf % FFN_SUB == 0 and n_j >= 2
    tile = pl.BlockSpec((tm, d), lambda i, j: (i, 0))
    tile_ahead = pl.BlockSpec((tm, d), lambda i, j: (jnp.minimum(i + (j + 1) // n_j, n_tiles - 1), 0))
    return pl.pallas_call(
        functools.partial(_ffn_kernel, final_norm=final_norm),
        grid=(n_tiles, n_j),
        in_specs=[
            tile_ahead,
            _resident((1, d)),
            pl.BlockSpec((d, tf), lambda i, j: (0, j)),
            pl.BlockSpec((tf, d), lambda i, j: (j, 0)),
            _resident((1, d)),
        ],
        out_specs=tile,
        out_shape=jax.ShapeDtypeStruct((m, d), _f32),
        scratch_shapes=[
            pltpu.VMEM((2, tm, d), _bf16),
        ],
        compiler_params=pltpu.CompilerParams(
            dimension_semantics=("arbitrary", "arbitrary"),
            vmem_limit_bytes=VMEM_LIMIT_BYTES),
        name="ffn_layer",
    )(xf, g_ffn.reshape(1, d), w_up_bf, w_down_bf, g_final.reshape(1, d))


def kernel(x, g_mix, w_in, g_v, w_s, b_s, w_pool, pool_scale, w_out, g_ffn, w_up, w_down, g_final):
    bsz, seq_len, d = x.shape
    depth = g_mix.shape[0]
    xf = x.reshape(bsz * seq_len, d)
    for layer in range(depth):
        xf, w_up_bf, w_down_bf = _mix_layer(
            xf, g_mix[layer], w_in[layer], g_v[layer], w_s[layer], b_s[layer], w_pool[layer],
            pool_scale[layer], w_out[layer], w_up[layer], w_down[layer], seq_len=seq_len)
        xf = _ffn_layer(xf, g_ffn[layer], w_up_bf, w_down_bf, g_final, final_norm=(layer == depth - 1))
    return xf.reshape(bsz, seq_len, d)
```

```python
import functools

import jax
import jax.numpy as jnp
from jax import lax
from jax.experimental import pallas as pl
from jax.experimental.pallas import tpu as pltpu

CHUNK = 128
A_HEAD_DIM = 128
POOL_WINDOWS = (2, 4, 8, 16)
EPS = 1e-6

HALO = 16
POOL_ROWS = 64
BF16_SUBLANES = 16
MIX_TM = 256
FFN_TM = 512
FFN_TF = 2048
FFN_SUB = 1024
VMEM_LIMIT_BYTES = 60 * 1024 * 1024

LOAD_ROWS = 256
LOAD_COLS = 1024
LOAD_SLOTS = 8

_bf16 = jnp.bfloat16
_f32 = jnp.float32


def _rms_scale(x):
    return lax.rsqrt(jnp.mean(x * x, axis=-1, keepdims=True) + EPS)


def _gelu_tanh(x):
    c = 0.7978845608028654
    t = jnp.tanh(x * ((x * x) * (c * 0.044715) + c))
    return x * (0.5 * t + 0.5)


def _dot(a, b):
    return jnp.dot(a, b, preferred_element_type=_f32)


class _WeightLoader:
    def __init__(self, col_blocks, stage, sem):
        self.pieces = [(src, dst, r, c) for src, dst, c in col_blocks for r in range(0, src.shape[0], LOAD_ROWS)]
        self.block_end = []
        for src, _, _ in col_blocks:
            self.block_end.append((self.block_end[-1] if self.block_end else 0) + src.shape[0] // LOAD_ROWS)
        self.stage, self.sem = stage, sem
        self.done = 0

    def _read(self, i):
        src, _, r, c = self.pieces[i]
        slot = i % LOAD_SLOTS
        return pltpu.make_async_copy(src.at[r:r + LOAD_ROWS, c:c + LOAD_COLS], self.stage.at[slot], self.sem.at[slot])

    def start(self):
        for i in range(min(LOAD_SLOTS - 1, len(self.pieces))):
            self._read(i).start()

    def need(self, n_blocks):
        while self.done < self.block_end[n_blocks - 1]:
            i = self.done
            if i + LOAD_SLOTS - 1 < len(self.pieces):
                self._read(i + LOAD_SLOTS - 1).start()
            self._read(i).wait()
            _, dst, r, c = self.pieces[i]
            dst[r:r + LOAD_ROWS, c:c + LOAD_COLS] = self.stage[i % LOAD_SLOTS].astype(_bf16)
            self.done += 1


def _mix_kernel(x_ref, gmix_ref, gv_ref, ws_ref, bst_ref, pscale_ref,
                win_hbm, wpool_hbm, wout_hbm, wup_ref, wdown_ref,
                o_ref, wup_bf_ref, wdown_bf_ref,
                win_ref, wpool_ref, wout_ref, zext_ref, pooled_ref, mixed_ref,
                stage_ref, pool_stage_ref, load_sem, pool_sem, *, seq_len):
    tm = x_ref.shape[0]
    a_width = gv_ref.shape[1]
    n_heads = a_width // A_HEAD_DIM
    group_dim = wpool_ref.shape[1]
    b_width = zext_ref.shape[1]
    u_col, v_col, z_col = 0, a_width, 2 * a_width
    step = pl.program_id(0)
    seq_pos0 = (step * tm) % seq_len
    next_seq_pos0 = ((step + 1) * tm) % seq_len

    def tile_body(first_step):
        if first_step:
            zext_ref[0:HALO, :] = jnp.zeros((HALO, zext_ref.shape[1]), _f32)
            pool_copy = pltpu.make_async_copy(wpool_hbm, pool_stage_ref, pool_sem.at[0])
            pool_copy.start()
            in_blocks = [(win_hbm, win_ref, c)
                         for c0, width in ((z_col, b_width), (v_col, a_width), (u_col, a_width))
                         for c in range(c0, c0 + width, LOAD_COLS)]
            out_blocks = [(wout_hbm, wout_ref, c) for c in range(0, wout_ref.shape[1], LOAD_COLS)]
            loader = _WeightLoader(in_blocks + out_blocks, stage_ref, load_sem)
            loader.start()
        z_blocks, a_blocks = b_width // LOAD_COLS, a_width // LOAD_COLS

        def weights_ready(n_blocks):
            if first_step:
                loader.need(n_blocks)

        wup_bf_ref[...] = wup_ref[...].astype(_bf16)
        wdown_bf_ref[...] = wdown_ref[...].astype(_bf16)

        x = x_ref[...]
        hb = (x * _rms_scale(x) * gmix_ref[...]).astype(_bf16)

        def in_proj(c0, c1):
            return _dot(hb, win_ref[:, c0:c1])

        def pool_group(g):
            win = POOL_WINDOWS[g]
            gs = slice(g * group_dim, (g + 1) * group_dim)
            for r0 in range(0, tm, POOL_ROWS):
                zb = zext_ref[r0:r0 + HALO + POOL_ROWS, gs]
                s = zb
                shift = 1
                while shift < win:
                    s = s + pltpu.roll(s, shift, axis=0)
                    shift *= 2
                if r0 + 1 >= win:
                    mean = s[HALO:] * (1.0 / win)
                else:
                    pos = seq_pos0 + r0 + lax.broadcasted_iota(jnp.int32, (POOL_ROWS, 1), 0)
                    mean = s[HALO:] / jnp.minimum(pos + 1, win).astype(_f32)
                pooled_ref[r0:r0 + POOL_ROWS, gs] = (mean - zb[HALO:]).astype(_bf16)

        def pool_project(g):
            gs = slice(g * group_dim, (g + 1) * group_dim)
            y = _dot(pooled_ref[:, gs], wpool_ref[g]) * pscale_ref[:, gs]
            mixed_ref[:, a_width + g * group_dim:a_width + (g + 1) * group_dim] = y.astype(_bf16)

        row = lax.broadcasted_iota(jnp.int32, (CHUNK, CHUNK), 0)
        col = lax.broadcasted_iota(jnp.int32, (CHUNK, CHUNK), 1)
        causal = row >= col

        def normed_heads(vpart, hd0):
            out = []
            for k in range(vpart.shape[1] // A_HEAD_DIM):
                vh = vpart[:, k * A_HEAD_DIM:(k + 1) * A_HEAD_DIM]
                cs = slice((hd0 + k) * A_HEAD_DIM, (hd0 + k + 1) * A_HEAD_DIM)
                out.append((vh * _rms_scale(vh) * gv_ref[:, cs]).astype(_bf16))
            return out

        def gate_heads(upart, vns, hd0):
            for k, vn in enumerate(vns):
                hd = hd0 + k
                w = jnp.where(causal, ws_ref[hd], 0.0).astype(_bf16)
                bias = bst_ref[:, hd:hd + 1]
                n_chunks = tm // CHUNK
                vn_wide = jnp.concatenate([vn[c * CHUNK:(c + 1) * CHUNK] for c in range(n_chunks)], axis=1)
                mixed_wide = _dot(w, vn_wide) + bias
                for c in range(n_chunks):
                    rs = slice(c * CHUNK, (c + 1) * CHUNK)
                    uh = upart[rs, k * A_HEAD_DIM:(k + 1) * A_HEAD_DIM]
                    mixed_ref[rs, hd * A_HEAD_DIM:(hd + 1) * A_HEAD_DIM] = (
                        uh * mixed_wide[:, c * A_HEAD_DIM:(c + 1) * A_HEAD_DIM]).astype(_bf16)

        half_a, half_b, half_h, half_g = a_width // 2, zext_ref.shape[1] // 2, n_heads // 2, len(POOL_WINDOWS) // 2
        weights_ready(z_blocks)
        zext_ref[HALO:HALO + tm, 0:half_b] = in_proj(z_col, z_col + half_b)
        zext_ref[HALO:HALO + tm, half_b:] = in_proj(z_col + half_b, z_col + 2 * half_b)
        for g in range(half_g):
            pool_group(g)
        if first_step:
            pool_copy.wait()
            wpool_ref[...] = pool_stage_ref[...].astype(_bf16)
        for g in range(half_g):
            pool_project(g)
        weights_ready(z_blocks + a_blocks)
        v0 = _gelu_tanh(in_proj(v_col, v_col + half_a))
        for g in range(half_g, 2 * half_g):
            pool_group(g)
        zext_ref[0:HALO, :] = jnp.where(next_seq_pos0 == 0, 0.0, zext_ref[tm:tm + HALO, :])
        for g in range(half_g, 2 * half_g):
            pool_project(g)
        v1 = _gelu_tanh(in_proj(v_col + half_a, v_col + 2 * half_a))
        vn0 = normed_heads(v0, 0)
        weights_ready(z_blocks + 2 * a_blocks)
        u0 = _gelu_tanh(in_proj(u_col, u_col + half_a))
        vn1 = normed_heads(v1, half_h)
        gate_heads(u0, vn0, 0)
        u1 = _gelu_tanh(in_proj(u_col + half_a, u_col + 2 * half_a))
        gate_heads(u1, vn1, half_h)

        weights_ready(z_blocks + 2 * a_blocks + wout_ref.shape[1] // LOAD_COLS)
        o_ref[...] = x + _dot(mixed_ref[...], wout_ref[...])

    pl.when(step == 0)(functools.partial(tile_body, True))
    pl.when(step > 0)(functools.partial(tile_body, False))


def _ffn_kernel(x_ref, gffn_ref, wup_ref, wdown_ref, gfin_ref, o_ref, h_ref, *, final_norm):
    i, j = pl.program_id(0), pl.program_id(1)
    n_j = pl.num_programs(1)
    n_sub = wup_ref.shape[1] // FFN_SUB
    h_cur, h_next = h_ref.at[i % 2], h_ref.at[(i + 1) % 2]

    tm = x_ref.shape[0]

    def normalise_input(dst, r0=0, r1=tm):
        x = x_ref[r0:r1, :]
        dst[r0:r1, :] = (x * _rms_scale(x) * gffn_ref[...]).astype(_bf16)

    @pl.when((i == 0) & (j == 0))
    def _():
        normalise_input(h_cur)

    def step(first, last):
        def up(k):
            cols = slice(k * FFN_SUB, (k + 1) * FFN_SUB)
            return jnp.square(jnp.maximum(_dot(h_cur[...], wup_ref[:, cols]), 0.0)).astype(_bf16)

        def down(k, act):
            d = _dot(act, wdown_ref[k * FFN_SUB:(k + 1) * FFN_SUB, :])
            if first and k == 0:
                o_ref[...] = x_ref[...] + d
            elif last and k == n_sub - 1 and final_norm:
                y = o_ref[...] + d
                o_ref[...] = y * _rms_scale(y) * gfin_ref[...]
            else:
                o_ref[...] += d

        act = up(0)
        for k in range(1, n_sub):
            if last:
                rows = tm // (n_sub - 1) // BF16_SUBLANES * BF16_SUBLANES
                normalise_input(h_next, (k - 1) * rows, tm if k == n_sub - 1 else k * rows)
            nxt = up(k)
            down(k - 1, act)
            act = nxt
        down(n_sub - 1, act)

    pl.when(j == 0)(functools.partial(step, True, False))
    pl.when((j > 0) & (j < n_j - 1))(functools.partial(step, False, False))
    pl.when(j == n_j - 1)(functools.partial(step, False, True))


def _resident(shape):
    zeros = (0,) * len(shape)
    return pl.BlockSpec(shape, lambda *_: zeros, pipeline_mode=pl.Buffered(1))


def _row_slab(w, n_steps):
    rows = w.shape[0] // n_steps
    assert w.shape[0] % n_steps == 0 and rows % BF16_SUBLANES == 0
    return pl.BlockSpec((rows, w.shape[1]), lambda i: (i, 0))


def _mix_layer(xf, g_mix, w_in, g_v, w_s, b_s, w_pool, pool_scale, w_out, w_up, w_down, *, seq_len):
    m, d = xf.shape
    a_width = g_v.shape[0]
    b_width = pool_scale.shape[0]
    tm = MIX_TM
    n_steps = m // tm
    assert m % tm == 0 and seq_len % tm == 0 and tm % CHUNK == 0 and tm % POOL_ROWS == 0 and tm >= HALO
    for w in (w_in, w_out):
        assert w.shape[0] % LOAD_ROWS == 0 and w.shape[1] % LOAD_COLS == 0
    assert a_width % LOAD_COLS == 0 and b_width % LOAD_COLS == 0
    tile = pl.BlockSpec((tm, d), lambda i: (i, 0))
    hbm = pl.BlockSpec(memory_space=pl.ANY)
    small = (
        g_mix.reshape(1, d),
        g_v.reshape(1, a_width),
        w_s,
        b_s.T,
        pool_scale.reshape(1, b_width),
    )
    slabs = [_row_slab(w_up, n_steps), _row_slab(w_down, n_steps)]
    return pl.pallas_call(
        functools.partial(_mix_kernel, seq_len=seq_len),
        grid=(n_steps,),
        in_specs=[tile] + [_resident(op.shape) for op in small] + [hbm, hbm, hbm] + slabs,
        out_specs=[tile] + slabs,
        out_shape=[jax.ShapeDtypeStruct((m, d), _f32),
                   jax.ShapeDtypeStruct(w_up.shape, _bf16),
                   jax.ShapeDtypeStruct(w_down.shape, _bf16)],
        scratch_shapes=[
            pltpu.VMEM(w_in.shape, _bf16),
            pltpu.VMEM(w_pool.shape, _bf16),
            pltpu.VMEM(w_out.shape, _bf16),
            pltpu.VMEM((HALO + tm, b_width), _f32),
            pltpu.VMEM((tm, b_width), _bf16),
            pltpu.VMEM((tm, a_width + b_width), _bf16),
            pltpu.VMEM((LOAD_SLOTS, LOAD_ROWS, LOAD_COLS), _f32),
            pltpu.VMEM(w_pool.shape, _f32),
            pltpu.SemaphoreType.DMA((LOAD_SLOTS,)),
            pltpu.SemaphoreType.DMA((1,)),
        ],
        compiler_params=pltpu.CompilerParams(
            dimension_semantics=("arbitrary",),
            vmem_limit_bytes=VMEM_LIMIT_BYTES),
        name="mix_layer",
    )(xf, *small, w_in, w_pool, w_out, w_up, w_down)


def _ffn_layer(xf, g_ffn, w_up_bf, w_down_bf, g_final, *, final_norm):
    m, d = xf.shape
    d_ff = w_up_bf.shape[1]
    tm, tf = FFN_TM, FFN_TF
    n_tiles, n_j = m // tm, d_ff // tf
    assert m % tm == 0 and d_ff % tf == 0 and tf % FFN_SUB == 0 and n_j >= 2
    tile = pl.BlockSpec((tm, d), lambda i, j: (i, 0))
    tile_ahead = pl.BlockSpec((tm, d), lambda i, j: (jnp.minimum(i + (j + 1) // n_j, n_tiles - 1), 0))
    return pl.pallas_call(
        functools.partial(_ffn_kernel, final_norm=final_norm),
        grid=(n_tiles, n_j),
        in_specs=[
            tile_ahead,
            _resident((1, d)),
            pl.BlockSpec((d, tf), lambda i, j: (0, j)),
            pl.BlockSpec((tf, d), lambda i, j: (j, 0)),
            _resident((1, d)),
        ],
        out_specs=tile,
        out_shape=jax.ShapeDtypeStruct((m, d), _f32),
        scratch_shapes=[
            pltpu.VMEM((2, tm, d), _bf16),
        ],
        compiler_params=pltpu.CompilerParams(
            dimension_semantics=("arbitrary", "arbitrary"),
            vmem_limit_bytes=VMEM_LIMIT_BYTES),
        name="ffn_layer",
    )(xf, g_ffn.reshape(1, d), w_up_bf, w_down_bf, g_final.reshape(1, d))


def kernel(x, g_mix, w_in, g_v, w_s, b_s, w_pool, pool_scale, w_out, g_ffn, w_up, w_down, g_final):
    bsz, seq_len, d = x.shape
    depth = g_mix.shape[0]
    xf = x.reshape(bsz * seq_len, d)
    for layer in range(depth):
        xf, w_up_bf, w_down_bf = _mix_layer(
            xf, g_mix[layer], w_in[layer], g_v[layer], w_s[layer], b_s[layer], w_pool[layer],
            pool_scale[layer], w_out[layer], w_up[layer], w_down[layer], seq_len=seq_len)
        xf = _ffn_layer(xf, g_ffn[layer], w_up_bf, w_down_bf, g_final, final_norm=(layer == depth - 1))
    return xf.reshape(bsz, seq_len, d)
```

```python
import functools

import jax
import jax.numpy as jnp
from jax import lax
from jax.experimental import pallas as pl
from jax.experimental.pallas import tpu as pltpu

CHUNK = 128
A_HEAD_DIM = 128
POOL_WINDOWS = (2, 4, 8, 16)
EPS = 1e-6

HALO = 16
POOL_ROWS = 64
BF16_SUBLANES = 16
MIX_TM = 256
FFN_TM = 512
FFN_TF = 2048
FFN_SUB = 512
VMEM_LIMIT_BYTES = 60 * 1024 * 1024

LOAD_ROWS = 256
LOAD_COLS = 1024
LOAD_SLOTS = 8

_bf16 = jnp.bfloat16
_f32 = jnp.float32


def _rms_scale(x):
    return lax.rsqrt(jnp.mean(x * x, axis=-1, keepdims=True) + EPS)


def _gelu_tanh(x):
    c = 0.7978845608028654
    t = jnp.tanh(x * ((x * x) * (c * 0.044715) + c))
    return x * (0.5 * t + 0.5)


def _dot(a, b):
    return jnp.dot(a, b, preferred_element_type=_f32)


class _WeightLoader:
    def __init__(self, col_blocks, stage, sem):
        self.pieces = [(src, dst, r, c) for src, dst, c in col_blocks for r in range(0, src.shape[0], LOAD_ROWS)]
        self.block_end = []
        for src, _, _ in col_blocks:
            self.block_end.append((self.block_end[-1] if self.block_end else 0) + src.shape[0] // LOAD_ROWS)
        self.stage, self.sem = stage, sem
        self.done = 0

    def _read(self, i):
        src, _, r, c = self.pieces[i]
        slot = i % LOAD_SLOTS
        return pltpu.make_async_copy(src.at[r:r + LOAD_ROWS, c:c + LOAD_COLS], self.stage.at[slot], self.sem.at[slot])

    def start(self):
        for i in range(min(LOAD_SLOTS - 1, len(self.pieces))):
            self._read(i).start()

    def need(self, n_blocks):
        while self.done < self.block_end[n_blocks - 1]:
            i = self.done
            if i + LOAD_SLOTS - 1 < len(self.pieces):
                self._read(i + LOAD_SLOTS - 1).start()
            self._read(i).wait()
            _, dst, r, c = self.pieces[i]
            dst[r:r + LOAD_ROWS, c:c + LOAD_COLS] = self.stage[i % LOAD_SLOTS].astype(_bf16)
            self.done += 1


def _mix_kernel(x_ref, gmix_ref, gv_ref, ws_ref, bst_ref, pscale_ref,
                win_hbm, wpool_hbm, wout_hbm, wup_ref, wdown_ref,
                o_ref, wup_bf_ref, wdown_bf_ref,
                win_ref, wpool_ref, wout_ref, zext_ref, pooled_ref, mixed_ref,
                stage_ref, pool_stage_ref, load_sem, pool_sem, *, seq_len):
    tm = x_ref.shape[0]
    a_width = gv_ref.shape[1]
    n_heads = a_width // A_HEAD_DIM
    group_dim = wpool_ref.shape[1]
    b_width = zext_ref.shape[1]
    u_col, v_col, z_col = 0, a_width, 2 * a_width
    step = pl.program_id(0)
    seq_pos0 = (step * tm) % seq_len
    next_seq_pos0 = ((step + 1) * tm) % seq_len

    def tile_body(first_step):
        if first_step:
            zext_ref[0:HALO, :] = jnp.zeros((HALO, zext_ref.shape[1]), _f32)
            pool_copy = pltpu.make_async_copy(wpool_hbm, pool_stage_ref, pool_sem.at[0])
            pool_copy.start()
            in_blocks = [(win_hbm, win_ref, c)
                         for c0, width in ((z_col, b_width), (v_col, a_width), (u_col, a_width))
                         for c in range(c0, c0 + width, LOAD_COLS)]
            out_blocks = [(wout_hbm, wout_ref, c) for c in range(0, wout_ref.shape[1], LOAD_COLS)]
            loader = _WeightLoader(in_blocks + out_blocks, stage_ref, load_sem)
            loader.start()
        z_blocks, a_blocks = b_width // LOAD_COLS, a_width // LOAD_COLS

        def weights_ready(n_blocks):
            if first_step:
                loader.need(n_blocks)

        def convert_mlp_weights(k, n=4):
            for src, dst in ((wup_ref, wup_bf_ref), (wdown_ref, wdown_bf_ref)):
                cols = src.shape[1] // n
                dst[:, k * cols:(k + 1) * cols] = src[:, k * cols:(k + 1) * cols].astype(_bf16)

        x = x_ref[...]
        hb = (x * _rms_scale(x) * gmix_ref[...]).astype(_bf16)

        def in_proj(c0, c1):
            return _dot(hb, win_ref[:, c0:c1])

        def pool_group(g):
            win = POOL_WINDOWS[g]
            gs = slice(g * group_dim, (g + 1) * group_dim)
            for r0 in range(0, tm, POOL_ROWS):
                zb = zext_ref[r0:r0 + HALO + POOL_ROWS, gs]
                s = zb
                shift = 1
                while shift < win:
                    s = s + pltpu.roll(s, shift, axis=0)
                    shift *= 2
                if r0 + 1 >= win:
                    mean = s[HALO:] * (1.0 / win)
                else:
                    pos = seq_pos0 + r0 + lax.broadcasted_iota(jnp.int32, (POOL_ROWS, 1), 0)
                    mean = s[HALO:] / jnp.minimum(pos + 1, win).astype(_f32)
                pooled_ref[r0:r0 + POOL_ROWS, gs] = (mean - zb[HALO:]).astype(_bf16)

        def pool_project(g):
            gs = slice(g * group_dim, (g + 1) * group_dim)
            y = _dot(pooled_ref[:, gs], wpool_ref[g]) * pscale_ref[:, gs]
            mixed_ref[:, a_width + g * group_dim:a_width + (g + 1) * group_dim] = y.astype(_bf16)

        row = lax.broadcasted_iota(jnp.int32, (CHUNK, CHUNK), 0)
        col = lax.broadcasted_iota(jnp.int32, (CHUNK, CHUNK), 1)
        causal = row >= col

        def normed_heads(vpart, hd0):
            out = []
            for k in range(vpart.shape[1] // A_HEAD_DIM):
                vh = vpart[:, k * A_HEAD_DIM:(k + 1) * A_HEAD_DIM]
                cs = slice((hd0 + k) * A_HEAD_DIM, (hd0 + k + 1) * A_HEAD_DIM)
                out.append((vh * _rms_scale(vh) * gv_ref[:, cs]).astype(_bf16))
            return out

        def gate_heads(upart, vns, hd0):
            for k, vn in enumerate(vns):
                hd = hd0 + k
                w = jnp.where(causal, ws_ref[hd], 0.0).astype(_bf16)
                bias = bst_ref[:, hd:hd + 1]
                n_chunks = tm // CHUNK
                vn_wide = jnp.concatenate([vn[c * CHUNK:(c + 1) * CHUNK] for c in range(n_chunks)], axis=1)
                mixed_wide = _dot(w, vn_wide) + bias
                for c in range(n_chunks):
                    rs = slice(c * CHUNK, (c + 1) * CHUNK)
                    uh = upart[rs, k * A_HEAD_DIM:(k + 1) * A_HEAD_DIM]
                    mixed_ref[rs, hd * A_HEAD_DIM:(hd + 1) * A_HEAD_DIM] = (
                        uh * mixed_wide[:, c * A_HEAD_DIM:(c + 1) * A_HEAD_DIM]).astype(_bf16)

        half_a, half_b, half_h, half_g = a_width // 2, zext_ref.shape[1] // 2, n_heads // 2, len(POOL_WINDOWS) // 2
        weights_ready(z_blocks)
        zext_ref[HALO:HALO + tm, 0:half_b] = in_proj(z_col, z_col + half_b)
        zext_ref[HALO:HALO + tm, half_b:] = in_proj(z_col + half_b, z_col + 2 * half_b)
        for g in range(half_g):
            pool_group(g)
        if first_step:
            pool_copy.wait()
            wpool_ref[...] = pool_stage_ref[...].astype(_bf16)
        for g in range(half_g):
            pool_project(g)
        convert_mlp_weights(0)
        weights_ready(z_blocks + a_blocks)
        v0 = _gelu_tanh(in_proj(v_col, v_col + half_a))
        convert_mlp_weights(1)
        for g in range(half_g, 2 * half_g):
            pool_group(g)
        zext_ref[0:HALO, :] = jnp.where(next_seq_pos0 == 0, 0.0, zext_ref[tm:tm + HALO, :])
        for g in range(half_g, 2 * half_g):
            pool_project(g)
        v1 = _gelu_tanh(in_proj(v_col + half_a, v_col + 2 * half_a))
        convert_mlp_weights(2)
        vn0 = normed_heads(v0, 0)
        weights_ready(z_blocks + 2 * a_blocks)
        u0 = _gelu_tanh(in_proj(u_col, u_col + half_a))
        convert_mlp_weights(3)
        vn1 = normed_heads(v1, half_h)
        gate_heads(u0, vn0, 0)
        u1 = _gelu_tanh(in_proj(u_col + half_a, u_col + 2 * half_a))
        gate_heads(u1, vn1, half_h)

        weights_ready(z_blocks + 2 * a_blocks + wout_ref.shape[1] // LOAD_COLS)
        o_ref[...] = x + _dot(mixed_ref[...], wout_ref[...])

    pl.when(step == 0)(functools.partial(tile_body, True))
    pl.when(step > 0)(functools.partial(tile_body, False))


def _ffn_kernel(x_ref, gffn_ref, wup_ref, wdown_ref, gfin_ref, o_ref, h_ref, *, final_norm):
    i, j = pl.program_id(0), pl.program_id(1)
    n_j = pl.num_programs(1)
    n_sub = wup_ref.shape[1] // FFN_SUB
    h_cur, h_next = h_ref.at[i % 2], h_ref.at[(i + 1) % 2]

    tm = x_ref.shape[0]

    def normalise_input(dst, r0=0, r1=tm):
        x = x_ref[r0:r1, :]
        dst[r0:r1, :] = (x * _rms_scale(x) * gffn_ref[...]).astype(_bf16)

    @pl.when((i == 0) & (j == 0))
    def _():
        normalise_input(h_cur)

    def step(first, last):
        def up(k):
            cols = slice(k * FFN_SUB, (k + 1) * FFN_SUB)
            return jnp.square(jnp.maximum(_dot(h_cur[...], wup_ref[:, cols]), 0.0)).astype(_bf16)

        def down(k, act):
            d = _dot(act, wdown_ref[k * FFN_SUB:(k + 1) * FFN_SUB, :])
            if first and k == 0:
                o_ref[...] = x_ref[...] + d
            elif last and k == n_sub - 1 and final_norm:
                y = o_ref[...] + d
                o_ref[...] = y * _rms_scale(y) * gfin_ref[...]
            else:
                o_ref[...] += d

        act = up(0)
        for k in range(1, n_sub):
            if last:
                rows = tm // (n_sub - 1) // BF16_SUBLANES * BF16_SUBLANES
                normalise_input(h_next, (k - 1) * rows, tm if k == n_sub - 1 else k * rows)
            nxt = up(k)
            down(k - 1, act)
            act = nxt
        down(n_sub - 1, act)

    pl.when(j == 0)(functools.partial(step, True, False))
    pl.when((j > 0) & (j < n_j - 1))(functools.partial(step, False, False))
    pl.when(j == n_j - 1)(functools.partial(step, False, True))


def _resident(shape):
    zeros = (0,) * len(shape)
    return pl.BlockSpec(shape, lambda *_: zeros, pipeline_mode=pl.Buffered(1))


def _row_slab(w, n_steps):
    rows = w.shape[0] // n_steps
    assert w.shape[0] % n_steps == 0 and rows % BF16_SUBLANES == 0
    return pl.BlockSpec((rows, w.shape[1]), lambda i: (i, 0))


def _mix_layer(xf, g_mix, w_in, g_v, w_s, b_s, w_pool, pool_scale, w_out, w_up, w_down, *, seq_len):
    m, d = xf.shape
    a_width = g_v.shape[0]
    b_width = pool_scale.shape[0]
    tm = MIX_TM
    n_steps = m // tm
    assert m % tm == 0 and seq_len % tm == 0 and tm % CHUNK == 0 and tm % POOL_ROWS == 0 and tm >= HALO
    for w in (w_in, w_out):
        assert w.shape[0] % LOAD_ROWS == 0 and w.shape[1] % LOAD_COLS == 0
    assert a_width % LOAD_COLS == 0 and b_width % LOAD_COLS == 0
    tile = pl.BlockSpec((tm, d), lambda i: (i, 0))
    hbm = pl.BlockSpec(memory_space=pl.ANY)
    small = (
        g_mix.reshape(1, d),
        g_v.reshape(1, a_width),
        w_s,
        b_s.T,
        pool_scale.reshape(1, b_width),
    )
    slabs = [_row_slab(w_up, n_steps), _row_slab(w_down, n_steps)]
    return pl.pallas_call(
        functools.partial(_mix_kernel, seq_len=seq_len),
        grid=(n_steps,),
        in_specs=[tile] + [_resident(op.shape) for op in small] + [hbm, hbm, hbm] + slabs,
        out_specs=[tile] + slabs,
        out_shape=[jax.ShapeDtypeStruct((m, d), _f32),
                   jax.ShapeDtypeStruct(w_up.shape, _bf16),
                   jax.ShapeDtypeStruct(w_down.shape, _bf16)],
        scratch_shapes=[
            pltpu.VMEM(w_in.shape, _bf16),
            pltpu.VMEM(w_pool.shape, _bf16),
            pltpu.VMEM(w_out.shape, _bf16),
            pltpu.VMEM((HALO + tm, b_width), _f32),
            pltpu.VMEM((tm, b_width), _bf16),
            pltpu.VMEM((tm, a_width + b_width), _bf16),
            pltpu.VMEM((LOAD_SLOTS, LOAD_ROWS, LOAD_COLS), _f32),
            pltpu.VMEM(w_pool.shape, _f32),
            pltpu.SemaphoreType.DMA((LOAD_SLOTS,)),
            pltpu.SemaphoreType.DMA((1,)),
        ],
        compiler_params=pltpu.CompilerParams(
            dimension_semantics=("arbitrary",),
            vmem_limit_bytes=VMEM_LIMIT_BYTES),
        name="mix_layer",
    )(xf, *small, w_in, w_pool, w_out, w_up, w_down)


def _ffn_layer(xf, g_ffn, w_up_bf, w_down_bf, g_final, *, final_norm):
    m, d = xf.shape
    d_ff = w_up_bf.shape[1]
    tm, tf = FFN_TM, FFN_TF
    n_tiles, n_j = m // tm, d_ff // tf
    assert m % tm == 0 and d_ff % tf == 0 and tf % FFN_SUB == 0 and n_j >= 2
    tile = pl.BlockSpec((tm, d), lambda i, j: (i, 0))
    tile_ahead = pl.BlockSpec((tm, d), lambda i, j: (jnp.minimum(i + (j + 1) // n_j, n_tiles - 1), 0))
    return pl.pallas_call(
        functools.partial(_ffn_kernel, final_norm=final_norm),
        grid=(n_tiles, n_j),
        in_specs=[
            tile_ahead,
            _resident((1, d)),
            pl.BlockSpec((d, tf), lambda i, j: (0, j)),
            pl.BlockSpec((tf, d), lambda i, j: (j, 0)),
            _resident((1, d)),
        ],
        out_specs=tile,
        out_shape=jax.ShapeDtypeStruct((m, d), _f32),
        scratch_shapes=[
            pltpu.VMEM((2, tm, d), _bf16),
        ],
        compiler_params=pltpu.CompilerParams(
            dimension_semantics=("arbitrary", "arbitrary"),
            vmem_limit_bytes=VMEM_LIMIT_BYTES),
        name="ffn_layer",
    )(xf, g_ffn.reshape(1, d), w_up_bf, w_down_bf, g_final.reshape(1, d))


def kernel(x, g_mix, w_in, g_v, w_s, b_s, w_pool, pool_scale, w_out, g_ffn, w_up, w_down, g_final):
    bsz, seq_len, d = x.shape
    depth = g_mix.shape[0]
    xf = x.reshape(bsz * seq_len, d)
    for layer in range(depth):
        xf, w_up_bf, w_down_bf = _mix_layer(
            xf, g_mix[layer], w_in[layer], g_v[layer], w_s[layer], b_s[layer], w_pool[layer],
            pool_scale[layer], w_out[layer], w_up[layer], w_down[layer], seq_len=seq_len)
        xf = _ffn_layer(xf, g_ffn[layer], w_up_bf, w_down_bf, g_final, final_norm=(layer == depth - 1))
    return xf.reshape(bsz, seq_len, d)
```

```python
import functools

import jax
import jax.numpy as jnp
from jax import lax
from jax.experimental import pallas as pl
from jax.experimental.pallas import tpu as pltpu

CHUNK = 128
A_HEAD_DIM = 128
POOL_WINDOWS = (2, 4, 8, 16)
EPS = 1e-6

HALO = 16
POOL_ROWS = 64
BF16_SUBLANES = 16
MIX_TM = 256
FFN_TM = 512
FFN_TF = 2048
FFN_SUB = 256
NORM_SHARES = 3
VMEM_LIMIT_BYTES = 60 * 1024 * 1024

LOAD_ROWS = 256
LOAD_COLS = 1024
LOAD_SLOTS = 8

_bf16 = jnp.bfloat16
_f32 = jnp.float32


def _rms_scale(x):
    return lax.rsqrt(jnp.mean(x * x, axis=-1, keepdims=True) + EPS)


def _gelu_tanh(x):
    c = 0.7978845608028654
    t = jnp.tanh(x * ((x * x) * (c * 0.044715) + c))
    return x * (0.5 * t + 0.5)


def _dot(a, b):
    return jnp.dot(a, b, preferred_element_type=_f32)


class _WeightLoader:
    def __init__(self, col_blocks, stage, sem):
        self.pieces = [(src, dst, r, c) for src, dst, c in col_blocks for r in range(0, src.shape[0], LOAD_ROWS)]
        self.block_end = []
        for src, _, _ in col_blocks:
            self.block_end.append((self.block_end[-1] if self.block_end else 0) + src.shape[0] // LOAD_ROWS)
        self.stage, self.sem = stage, sem
        self.done = 0

    def _read(self, i):
        src, _, r, c = self.pieces[i]
        slot = i % LOAD_SLOTS
        return pltpu.make_async_copy(src.at[r:r + LOAD_ROWS, c:c + LOAD_COLS], self.stage.at[slot], self.sem.at[slot])

    def start(self):
        for i in range(min(LOAD_SLOTS - 1, len(self.pieces))):
            self._read(i).start()

    def need(self, n_blocks):
        while self.done < self.block_end[n_blocks - 1]:
            i = self.done
            if i + LOAD_SLOTS - 1 < len(self.pieces):
                self._read(i + LOAD_SLOTS - 1).start()
            self._read(i).wait()
            _, dst, r, c = self.pieces[i]
            dst[r:r + LOAD_ROWS, c:c + LOAD_COLS] = self.stage[i % LOAD_SLOTS].astype(_bf16)
            self.done += 1


def _mix_kernel(x_ref, gmix_ref, gv_ref, ws_ref, bst_ref, pscale_ref,
                win_hbm, wpool_hbm, wout_hbm, wup_ref, wdown_ref,
                o_ref, wup_bf_ref, wdown_bf_ref,
                win_ref, wpool_ref, wout_ref, zext_ref, pooled_ref, mixed_ref,
                stage_ref, pool_stage_ref, load_sem, pool_sem, *, seq_len):
    tm = x_ref.shape[0]
    a_width = gv_ref.shape[1]
    n_heads = a_width // A_HEAD_DIM
    group_dim = wpool_ref.shape[1]
    b_width = zext_ref.shape[1]
    u_col, v_col, z_col = 0, a_width, 2 * a_width
    step = pl.program_id(0)
    seq_pos0 = (step * tm) % seq_len
    next_seq_pos0 = ((step + 1) * tm) % seq_len

    def tile_body(first_step):
        if first_step:
            zext_ref[0:HALO, :] = jnp.zeros((HALO, zext_ref.shape[1]), _f32)
            pool_copy = pltpu.make_async_copy(wpool_hbm, pool_stage_ref, pool_sem.at[0])
            pool_copy.start()
            in_blocks = [(win_hbm, win_ref, c)
                         for c0, width in ((z_col, b_width), (v_col, a_width), (u_col, a_width))
                         for c in range(c0, c0 + width, LOAD_COLS)]
            out_blocks = [(wout_hbm, wout_ref, c) for c in range(0, wout_ref.shape[1], LOAD_COLS)]
            loader = _WeightLoader(in_blocks + out_blocks, stage_ref, load_sem)
            loader.start()
        z_blocks, a_blocks = b_width // LOAD_COLS, a_width // LOAD_COLS

        def weights_ready(n_blocks):
            if first_step:
                loader.need(n_blocks)

        def convert_mlp_weights(k, n=4):
            for src, dst in ((wup_ref, wup_bf_ref), (wdown_ref, wdown_bf_ref)):
                cols = src.shape[1] // n
                dst[:, k * cols:(k + 1) * cols] = src[:, k * cols:(k + 1) * cols].astype(_bf16)

        x = x_ref[...]
        hb = (x * _rms_scale(x) * gmix_ref[...]).astype(_bf16)

        def in_proj(c0, c1):
            return _dot(hb, win_ref[:, c0:c1])

        def pool_group(g):
            win = POOL_WINDOWS[g]
            gs = slice(g * group_dim, (g + 1) * group_dim)
            for r0 in range(0, tm, POOL_ROWS):
                zb = zext_ref[r0:r0 + HALO + POOL_ROWS, gs]
                s = zb
                shift = 1
                while shift < win:
                    s = s + pltpu.roll(s, shift, axis=0)
                    shift *= 2
                if r0 + 1 >= win:
                    mean = s[HALO:] * (1.0 / win)
                else:
                    pos = seq_pos0 + r0 + lax.broadcasted_iota(jnp.int32, (POOL_ROWS, 1), 0)
                    mean = s[HALO:] / jnp.minimum(pos + 1, win).astype(_f32)
                pooled_ref[r0:r0 + POOL_ROWS, gs] = (mean - zb[HALO:]).astype(_bf16)

        def pool_project(g):
            gs = slice(g * group_dim, (g + 1) * group_dim)
            y = _dot(pooled_ref[:, gs], wpool_ref[g]) * pscale_ref[:, gs]
            mixed_ref[:, a_width + g * group_dim:a_width + (g + 1) * group_dim] = y.astype(_bf16)

        row = lax.broadcasted_iota(jnp.int32, (CHUNK, CHUNK), 0)
        col = lax.broadcasted_iota(jnp.int32, (CHUNK, CHUNK), 1)
        causal = row >= col

        def normed_heads(vpart, hd0):
            out = []
            for k in range(vpart.shape[1] // A_HEAD_DIM):
                vh = vpart[:, k * A_HEAD_DIM:(k + 1) * A_HEAD_DIM]
                cs = slice((hd0 + k) * A_HEAD_DIM, (hd0 + k + 1) * A_HEAD_DIM)
                out.append((vh * _rms_scale(vh) * gv_ref[:, cs]).astype(_bf16))
            return out

        def gate_heads(upart, vns, hd0):
            for k, vn in enumerate(vns):
                hd = hd0 + k
                w = jnp.where(causal, ws_ref[hd], 0.0).astype(_bf16)
                bias = bst_ref[:, hd:hd + 1]
                n_chunks = tm // CHUNK
                vn_wide = jnp.concatenate([vn[c * CHUNK:(c + 1) * CHUNK] for c in range(n_chunks)], axis=1)
                mixed_wide = _dot(w, vn_wide) + bias
                for c in range(n_chunks):
                    rs = slice(c * CHUNK, (c + 1) * CHUNK)
                    uh = upart[rs, k * A_HEAD_DIM:(k + 1) * A_HEAD_DIM]
                    mixed_ref[rs, hd * A_HEAD_DIM:(hd + 1) * A_HEAD_DIM] = (
                        uh * mixed_wide[:, c * A_HEAD_DIM:(c + 1) * A_HEAD_DIM]).astype(_bf16)

        half_a, half_b, half_h, half_g = a_width // 2, zext_ref.shape[1] // 2, n_heads // 2, len(POOL_WINDOWS) // 2
        weights_ready(z_blocks)
        zext_ref[HALO:HALO + tm, 0:half_b] = in_proj(z_col, z_col + half_b)
        zext_ref[HALO:HALO + tm, half_b:] = in_proj(z_col + half_b, z_col + 2 * half_b)
        for g in range(half_g):
            pool_group(g)
        if first_step:
            pool_copy.wait()
            wpool_ref[...] = pool_stage_ref[...].astype(_bf16)
        for g in range(half_g):
            pool_project(g)
        convert_mlp_weights(0)
        weights_ready(z_blocks + a_blocks)
        v0 = _gelu_tanh(in_proj(v_col, v_col + half_a))
        convert_mlp_weights(1)
        for g in range(half_g, 2 * half_g):
            pool_group(g)
        zext_ref[0:HALO, :] = jnp.where(next_seq_pos0 == 0, 0.0, zext_ref[tm:tm + HALO, :])
        for g in range(half_g, 2 * half_g):
            pool_project(g)
        v1 = _gelu_tanh(in_proj(v_col + half_a, v_col + 2 * half_a))
        convert_mlp_weights(2)
        vn0 = normed_heads(v0, 0)
        weights_ready(z_blocks + 2 * a_blocks)
        u0 = _gelu_tanh(in_proj(u_col, u_col + half_a))
        convert_mlp_weights(3)
        vn1 = normed_heads(v1, half_h)
        gate_heads(u0, vn0, 0)
        u1 = _gelu_tanh(in_proj(u_col + half_a, u_col + 2 * half_a))
        gate_heads(u1, vn1, half_h)

        weights_ready(z_blocks + 2 * a_blocks + wout_ref.shape[1] // LOAD_COLS)
        o_ref[...] = x + _dot(mixed_ref[...], wout_ref[...])

    pl.when(step == 0)(functools.partial(tile_body, True))
    pl.when(step > 0)(functools.partial(tile_body, False))


def _ffn_kernel(x_ref, gffn_ref, wup_ref, wdown_ref, gfin_ref, o_ref, h_ref, *, final_norm):
    i, j = pl.program_id(0), pl.program_id(1)
    n_j = pl.num_programs(1)
    n_sub = wup_ref.shape[1] // FFN_SUB
    h_cur, h_next = h_ref.at[i % 2], h_ref.at[(i + 1) % 2]

    tm = x_ref.shape[0]

    def normalise_input(dst, r0=0, r1=tm):
        x = x_ref[r0:r1, :]
        dst[r0:r1, :] = (x * _rms_scale(x) * gffn_ref[...]).astype(_bf16)

    @pl.when((i == 0) & (j == 0))
    def _():
        normalise_input(h_cur)

    def step(first, last):
        def up(k):
            cols = slice(k * FFN_SUB, (k + 1) * FFN_SUB)
            return jnp.square(jnp.maximum(_dot(h_cur[...], wup_ref[:, cols]), 0.0)).astype(_bf16)

        def down(k, act):
            d = _dot(act, wdown_ref[k * FFN_SUB:(k + 1) * FFN_SUB, :])
            if first and k == 0:
                o_ref[...] = x_ref[...] + d
            elif last and k == n_sub - 1 and final_norm:
                y = o_ref[...] + d
                o_ref[...] = y * _rms_scale(y) * gfin_ref[...]
            else:
                o_ref[...] += d

        act = up(0)
        for k in range(1, n_sub):
            if last and k * (NORM_SHARES + 1) % n_sub == 0:
                s = k * (NORM_SHARES + 1) // n_sub - 1
                rows = tm // NORM_SHARES // BF16_SUBLANES * BF16_SUBLANES
                normalise_input(h_next, s * rows, tm if s == NORM_SHARES - 1 else (s + 1) * rows)
            nxt = up(k)
            down(k - 1, act)
            act = nxt
        down(n_sub - 1, act)

    pl.when(j == 0)(functools.partial(step, True, False))
    pl.when((j > 0) & (j < n_j - 1))(functools.partial(step, False, False))
    pl.when(j == n_j - 1)(functools.partial(step, False, True))


def _resident(shape):
    zeros = (0,) * len(shape)
    return pl.BlockSpec(shape, lambda *_: zeros, pipeline_mode=pl.Buffered(1))


def _row_slab(w, n_steps):
    rows = w.shape[0] // n_steps
    assert w.shape[0] % n_steps == 0 and rows % BF16_SUBLANES == 0
    return pl.BlockSpec((rows, w.shape[1]), lambda i: (i, 0))


def _mix_layer(xf, g_mix, w_in, g_v, w_s, b_s, w_pool, pool_scale, w_out, w_up, w_down, *, seq_len):
    m, d = xf.shape
    a_width = g_v.shape[0]
    b_width = pool_scale.shape[0]
    tm = MIX_TM
    n_steps = m // tm
    assert m % tm == 0 and seq_len % tm == 0 and tm % CHUNK == 0 and tm % POOL_ROWS == 0 and tm >= HALO
    for w in (w_in, w_out):
        assert w.shape[0] % LOAD_ROWS == 0 and w.shape[1] % LOAD_COLS == 0
    assert a_width % LOAD_COLS == 0 and b_width % LOAD_COLS == 0
    tile = pl.BlockSpec((tm, d), lambda i: (i, 0))
    hbm = pl.BlockSpec(memory_space=pl.ANY)
    small = (
        g_mix.reshape(1, d),
        g_v.reshape(1, a_width),
        w_s,
        b_s.T,
        pool_scale.reshape(1, b_width),
    )
    slabs = [_row_slab(w_up, n_steps), _row_slab(w_down, n_steps)]
    return pl.pallas_call(
        functools.partial(_mix_kernel, seq_len=seq_len),
        grid=(n_steps,),
        in_specs=[tile] + [_resident(op.shape) for op in small] + [hbm, hbm, hbm] + slabs,
        out_specs=[tile] + slabs,
        out_shape=[jax.ShapeDtypeStruct((m, d), _f32),
                   jax.ShapeDtypeStruct(w_up.shape, _bf16),
                   jax.ShapeDtypeStruct(w_down.shape, _bf16)],
        scratch_shapes=[
            pltpu.VMEM(w_in.shape, _bf16),
            pltpu.VMEM(w_pool.shape, _bf16),
            pltpu.VMEM(w_out.shape, _bf16),
            pltpu.VMEM((HALO + tm, b_width), _f32),
            pltpu.VMEM((tm, b_width), _bf16),
            pltpu.VMEM((tm, a_width + b_width), _bf16),
            pltpu.VMEM((LOAD_SLOTS, LOAD_ROWS, LOAD_COLS), _f32),
            pltpu.VMEM(w_pool.shape, _f32),
            pltpu.SemaphoreType.DMA((LOAD_SLOTS,)),
            pltpu.SemaphoreType.DMA((1,)),
        ],
        compiler_params=pltpu.CompilerParams(
            dimension_semantics=("arbitrary",),
            vmem_limit_bytes=VMEM_LIMIT_BYTES),
        name="mix_layer",
    )(xf, *small, w_in, w_pool, w_out, w_up, w_down)


def _ffn_layer(xf, g_ffn, w_up_bf, w_down_bf, g_final, *, final_norm):
    m, d = xf.shape
    d_ff = w_up_bf.shape[1]
    tm, tf = FFN_TM, FFN_TF
    n_tiles, n_j = m // tm, d_ff // tf
    assert m % tm == 0 and d_ff % tf == 0 and tf % FFN_SUB == 0 and n_j >= 2
    assert (tf // FFN_SUB) % (NORM_SHARES + 1) == 0
    tile = pl.BlockSpec((tm, d), lambda i, j: (i, 0))
    tile_ahead = pl.BlockSpec((tm, d), lambda i, j: (jnp.minimum(i + (j + 1) // n_j, n_tiles - 1), 0))
    return pl.pallas_call(
        functools.partial(_ffn_kernel, final_norm=final_norm),
        grid=(n_tiles, n_j),
        in_specs=[
            tile_ahead,
            _resident((1, d)),
            pl.BlockSpec((d, tf), lambda i, j: (0, j)),
            pl.BlockSpec((tf, d), lambda i, j: (j, 0)),
            _resident((1, d)),
        ],
        out_specs=tile,
        out_shape=jax.ShapeDtypeStruct((m, d), _f32),
        scratch_shapes=[
            pltpu.VMEM((2, tm, d), _bf16),
        ],
        compiler_params=pltpu.CompilerParams(
            dimension_semantics=("arbitrary", "arbitrary"),
            vmem_limit_bytes=VMEM_LIMIT_BYTES),
        name="ffn_layer",
    )(xf, g_ffn.reshape(1, d), w_up_bf, w_down_bf, g_final.reshape(1, d))


def kernel(x, g_mix, w_in, g_v, w_s, b_s, w_pool, pool_scale, w_out, g_ffn, w_up, w_down, g_final):
    bsz, seq_len, d = x.shape
    depth = g_mix.shape[0]
    xf = x.reshape(bsz * seq_len, d)
    for layer in range(depth):
        xf, w_up_bf, w_down_bf = _mix_layer(
            xf, g_mix[layer], w_in[layer], g_v[layer], w_s[layer], b_s[layer], w_pool[layer],
            pool_scale[layer], w_out[layer], w_up[layer], w_down[layer], seq_len=seq_len)
        xf = _ffn_layer(xf, g_ffn[layer], w_up_bf, w_down_bf, g_final, final_norm=(layer == depth - 1))
    return xf.reshape(bsz, seq_len, d)
```

```python
import functools

import jax
import jax.numpy as jnp
from jax import lax
from jax.experimental import pallas as pl
from jax.experimental.pallas import tpu as pltpu

CHUNK = 128
A_HEAD_DIM = 128
POOL_WINDOWS = (2, 4, 8, 16)
EPS = 1e-6

HALO = 16
POOL_ROWS = 64
BF16_SUBLANES = 16
MIX_TM = 256
FFN_TM = 512
FFN_TF = 2048
FFN_SUB = 512
VMEM_LIMIT_BYTES = 60 * 1024 * 1024

LOAD_ROWS = 256
LOAD_COLS = 1024
LOAD_SLOTS = 8

_bf16 = jnp.bfloat16
_f32 = jnp.float32


def _rms_scale(x):
    return lax.rsqrt(jnp.mean(x * x, axis=-1, keepdims=True) + EPS)


def _gelu_tanh(x):
    c = 0.7978845608028654
    t = jnp.tanh(x * ((x * x) * (c * 0.044715) + c))
    return x * (0.5 * t + 0.5)


def _dot(a, b):
    return jnp.dot(a, b, preferred_element_type=_f32)


class _WeightLoader:
    def __init__(self, col_blocks, stage, sem):
        self.pieces = [(src, dst, r, c) for src, dst, c in col_blocks for r in range(0, src.shape[0], LOAD_ROWS)]
        self.block_end = []
        for src, _, _ in col_blocks:
            self.block_end.append((self.block_end[-1] if self.block_end else 0) + src.shape[0] // LOAD_ROWS)
        self.stage, self.sem = stage, sem
        self.done = 0

    def _read(self, i):
        src, _, r, c = self.pieces[i]
        slot = i % LOAD_SLOTS
        return pltpu.make_async_copy(src.at[r:r + LOAD_ROWS, c:c + LOAD_COLS], self.stage.at[slot], self.sem.at[slot])

    def start(self):
        for i in range(min(LOAD_SLOTS - 1, len(self.pieces))):
            self._read(i).start()

    def need(self, n_blocks):
        while self.done < self.block_end[n_blocks - 1]:
            i = self.done
            if i + LOAD_SLOTS - 1 < len(self.pieces):
                self._read(i + LOAD_SLOTS - 1).start()
            self._read(i).wait()
            _, dst, r, c = self.pieces[i]
            dst[r:r + LOAD_ROWS, c:c + LOAD_COLS] = self.stage[i % LOAD_SLOTS].astype(_bf16)
            self.done += 1


def _mix_kernel(x_ref, gmix_ref, gv_ref, ws_ref, bst_ref, pscale_ref,
                win_hbm, wpool_hbm, wout_hbm, wup_ref, wdown_ref,
                o_ref, wup_bf_ref, wdown_bf_ref,
                win_ref, wpool_ref, wout_ref, zext_ref, pooled_ref, mixed_ref,
                stage_ref, pool_stage_ref, load_sem, pool_sem, *, seq_len):
    tm = x_ref.shape[0]
    a_width = gv_ref.shape[1]
    n_heads = a_width // A_HEAD_DIM
    group_dim = wpool_ref.shape[1]
    b_width = zext_ref.shape[1]
    u_col, v_col, z_col = 0, a_width, 2 * a_width
    step = pl.program_id(0)
    seq_pos0 = (step * tm) % seq_len
    next_seq_pos0 = ((step + 1) * tm) % seq_len

    def tile_body(first_step):
        if first_step:
            zext_ref[0:HALO, :] = jnp.zeros((HALO, zext_ref.shape[1]), _f32)
            pool_copy = pltpu.make_async_copy(wpool_hbm, pool_stage_ref, pool_sem.at[0])
            pool_copy.start()
            in_blocks = [(win_hbm, win_ref, c)
                         for c0, width in ((z_col, b_width), (v_col, a_width), (u_col, a_width))
                         for c in range(c0, c0 + width, LOAD_COLS)]
            out_blocks = [(wout_hbm, wout_ref, c) for c in range(0, wout_ref.shape[1], LOAD_COLS)]
            loader = _WeightLoader(in_blocks + out_blocks, stage_ref, load_sem)
            loader.start()
        z_blocks, a_blocks = b_width // LOAD_COLS, a_width // LOAD_COLS

        def weights_ready(n_blocks):
            if first_step:
                loader.need(n_blocks)

        def convert_mlp_weights(k, n=4):
            for src, dst in ((wup_ref, wup_bf_ref), (wdown_ref, wdown_bf_ref)):
                cols = src.shape[1] // n
                dst[:, k * cols:(k + 1) * cols] = src[:, k * cols:(k + 1) * cols].astype(_bf16)

        x = x_ref[...]
        hb = (x * _rms_scale(x) * gmix_ref[...]).astype(_bf16)

        def in_proj(c0, c1):
            return _dot(hb, win_ref[:, c0:c1])

        def pool_group(g):
            win = POOL_WINDOWS[g]
            gs = slice(g * group_dim, (g + 1) * group_dim)
            for r0 in range(0, tm, POOL_ROWS):
                zb = zext_ref[r0:r0 + HALO + POOL_ROWS, gs]
                s = zb
                shift = 1
                while shift < win:
                    s = s + pltpu.roll(s, shift, axis=0)
                    shift *= 2
                if r0 + 1 >= win:
                    mean = s[HALO:] * (1.0 / win)
                else:
                    pos = seq_pos0 + r0 + lax.broadcasted_iota(jnp.int32, (POOL_ROWS, 1), 0)
                    mean = s[HALO:] / jnp.minimum(pos + 1, win).astype(_f32)
                pooled_ref[r0:r0 + POOL_ROWS, gs] = (mean - zb[HALO:]).astype(_bf16)

        def pool_project(g):
            gs = slice(g * group_dim, (g + 1) * group_dim)
            y = _dot(pooled_ref[:, gs], wpool_ref[g]) * pscale_ref[:, gs]
            mixed_ref[:, a_width + g * group_dim:a_width + (g + 1) * group_dim] = y.astype(_bf16)

        row = lax.broadcasted_iota(jnp.int32, (CHUNK, CHUNK), 0)
        col = lax.broadcasted_iota(jnp.int32, (CHUNK, CHUNK), 1)
        causal = row >= col

        def normed_heads(vpart, hd0):
            out = []
            for k in range(vpart.shape[1] // A_HEAD_DIM):
                vh = vpart[:, k * A_HEAD_DIM:(k + 1) * A_HEAD_DIM]
                cs = slice((hd0 + k) * A_HEAD_DIM, (hd0 + k + 1) * A_HEAD_DIM)
                out.append((vh * _rms_scale(vh) * gv_ref[:, cs]).astype(_bf16))
            return out

        def gate_heads(upart, vns, hd0):
            for k, vn in enumerate(vns):
                hd = hd0 + k
                w = jnp.where(causal, ws_ref[hd], 0.0).astype(_bf16)
                bias = bst_ref[:, hd:hd + 1]
                n_chunks = tm // CHUNK
                vn_wide = jnp.concatenate([vn[c * CHUNK:(c + 1) * CHUNK] for c in range(n_chunks)], axis=1)
                mixed_wide = _dot(w, vn_wide) + bias
                for c in range(n_chunks):
                    rs = slice(c * CHUNK, (c + 1) * CHUNK)
                    uh = upart[rs, k * A_HEAD_DIM:(k + 1) * A_HEAD_DIM]
                    mixed_ref[rs, hd * A_HEAD_DIM:(hd + 1) * A_HEAD_DIM] = (
                        uh * mixed_wide[:, c * A_HEAD_DIM:(c + 1) * A_HEAD_DIM]).astype(_bf16)

        half_a, half_b, half_h, half_g = a_width // 2, zext_ref.shape[1] // 2, n_heads // 2, len(POOL_WINDOWS) // 2
        weights_ready(z_blocks)
        zext_ref[HALO:HALO + tm, 0:half_b] = in_proj(z_col, z_col + half_b)
        zext_ref[HALO:HALO + tm, half_b:] = in_proj(z_col + half_b, z_col + 2 * half_b)
        for g in range(half_g):
            pool_group(g)
        if first_step:
            pool_copy.wait()
            wpool_ref[...] = pool_stage_ref[...].astype(_bf16)
        for g in range(half_g):
            pool_project(g)
        convert_mlp_weights(0)
        weights_ready(z_blocks + a_blocks)
        v0 = _gelu_tanh(in_proj(v_col, v_col + half_a))
        convert_mlp_weights(1)
        for g in range(half_g, 2 * half_g):
            pool_group(g)
        zext_ref[0:HALO, :] = jnp.where(next_seq_pos0 == 0, 0.0, zext_ref[tm:tm + HALO, :])
        for g in range(half_g, 2 * half_g):
            pool_project(g)
        v1 = _gelu_tanh(in_proj(v_col + half_a, v_col + 2 * half_a))
        convert_mlp_weights(2)
        vn0 = normed_heads(v0, 0)
        weights_ready(z_blocks + 2 * a_blocks)
        u0 = _gelu_tanh(in_proj(u_col, u_col + half_a))
        convert_mlp_weights(3)
        vn1 = normed_heads(v1, half_h)
        gate_heads(u0, vn0, 0)
        u1 = _gelu_tanh(in_proj(u_col + half_a, u_col + 2 * half_a))
        gate_heads(u1, vn1, half_h)

        weights_ready(z_blocks + 2 * a_blocks + wout_ref.shape[1] // LOAD_COLS)
        o_ref[...] = x + _dot(mixed_ref[...], wout_ref[...])

    pl.when(step == 0)(functools.partial(tile_body, True))
    pl.when(step > 0)(functools.partial(tile_body, False))


def _ffn_kernel(x_ref, gffn_ref, wup_hbm, wdown_hbm, gfin_ref, o_ref, h_ref, wup_buf, wdown_buf, w_sem, *,
                final_norm):
    i, j = pl.program_id(0), pl.program_id(1)
    tf = wup_buf.shape[2]
    n_i, n_j = pl.num_programs(0), wup_hbm.shape[1] // tf
    n_sub = tf // FFN_SUB
    h_cur, h_next = h_ref.at[i % 2], h_ref.at[(i + 1) % 2]
    t = i * n_j + j
    slot = t % 2

    def weight_copies(jj, s):
        off = pl.multiple_of(jj * tf, tf)
        return (pltpu.make_async_copy(wup_hbm.at[:, pl.ds(off, tf)], wup_buf.at[s], w_sem.at[0, s]),
                pltpu.make_async_copy(wdown_hbm.at[pl.ds(off, tf), :], wdown_buf.at[s], w_sem.at[1, s]))

    @pl.when(t == 0)
    def _():
        for c in weight_copies(j, slot):
            c.start()

    @pl.when(t + 1 < n_i * n_j)
    def _():
        for c in weight_copies(lax.rem(j + 1, n_j), 1 - slot):
            c.start()

    for c in weight_copies(j, slot):
        c.wait()
    wup_ref, wdown_ref = wup_buf.at[slot], wdown_buf.at[slot]

    tm = x_ref.shape[0]

    def normalise_input(dst, r0=0, r1=tm):
        x = x_ref[r0:r1, :]
        dst[r0:r1, :] = (x * _rms_scale(x) * gffn_ref[...]).astype(_bf16)

    @pl.when((i == 0) & (j == 0))
    def _():
        normalise_input(h_cur)

    def step(first, last):
        def up(k):
            cols = slice(k * FFN_SUB, (k + 1) * FFN_SUB)
            return jnp.square(jnp.maximum(_dot(h_cur[...], wup_ref[:, cols]), 0.0)).astype(_bf16)

        def down(k, act):
            d = _dot(act, wdown_ref[k * FFN_SUB:(k + 1) * FFN_SUB, :])
            if first and k == 0:
                o_ref[...] = x_ref[...] + d
            elif last and k == n_sub - 1 and final_norm:
                y = o_ref[...] + d
                o_ref[...] = y * _rms_scale(y) * gfin_ref[...]
            else:
                o_ref[...] += d

        act = up(0)
        for k in range(1, n_sub):
            if last:
                rows = tm // (n_sub - 1) // BF16_SUBLANES * BF16_SUBLANES
                normalise_input(h_next, (k - 1) * rows, tm if k == n_sub - 1 else k * rows)
            nxt = up(k)
            down(k - 1, act)
            act = nxt
        down(n_sub - 1, act)

    pl.when(j == 0)(functools.partial(step, True, False))
    pl.when((j > 0) & (j < n_j - 1))(functools.partial(step, False, False))
    pl.when(j == n_j - 1)(functools.partial(step, False, True))


def _resident(shape):
    zeros = (0,) * len(shape)
    return pl.BlockSpec(shape, lambda *_: zeros, pipeline_mode=pl.Buffered(1))


def _row_slab(w, n_steps):
    rows = w.shape[0] // n_steps
    assert w.shape[0] % n_steps == 0 and rows % BF16_SUBLANES == 0
    return pl.BlockSpec((rows, w.shape[1]), lambda i: (i, 0))


def _mix_layer(xf, g_mix, w_in, g_v, w_s, b_s, w_pool, pool_scale, w_out, w_up, w_down, *, seq_len):
    m, d = xf.shape
    a_width = g_v.shape[0]
    b_width = pool_scale.shape[0]
    tm = MIX_TM
    n_steps = m // tm
    assert m % tm == 0 and seq_len % tm == 0 and tm % CHUNK == 0 and tm % POOL_ROWS == 0 and tm >= HALO
    for w in (w_in, w_out):
        assert w.shape[0] % LOAD_ROWS == 0 and w.shape[1] % LOAD_COLS == 0
    assert a_width % LOAD_COLS == 0 and b_width % LOAD_COLS == 0
    tile = pl.BlockSpec((tm, d), lambda i: (i, 0))
    hbm = pl.BlockSpec(memory_space=pl.ANY)
    small = (
        g_mix.reshape(1, d),
        g_v.reshape(1, a_width),
        w_s,
        b_s.T,
        pool_scale.reshape(1, b_width),
    )
    slabs = [_row_slab(w_up, n_steps), _row_slab(w_down, n_steps)]
    return pl.pallas_call(
        functools.partial(_mix_kernel, seq_len=seq_len),
        grid=(n_steps,),
        in_specs=[tile] + [_resident(op.shape) for op in small] + [hbm, hbm, hbm] + slabs,
        out_specs=[tile] + slabs,
        out_shape=[jax.ShapeDtypeStruct((m, d), _f32),
                   jax.ShapeDtypeStruct(w_up.shape, _bf16),
                   jax.ShapeDtypeStruct(w_down.shape, _bf16)],
        scratch_shapes=[
            pltpu.VMEM(w_in.shape, _bf16),
            pltpu.VMEM(w_pool.shape, _bf16),
            pltpu.VMEM(w_out.shape, _bf16),
            pltpu.VMEM((HALO + tm, b_width), _f32),
            pltpu.VMEM((tm, b_width), _bf16),
            pltpu.VMEM((tm, a_width + b_width), _bf16),
            pltpu.VMEM((LOAD_SLOTS, LOAD_ROWS, LOAD_COLS), _f32),
            pltpu.VMEM(w_pool.shape, _f32),
            pltpu.SemaphoreType.DMA((LOAD_SLOTS,)),
            pltpu.SemaphoreType.DMA((1,)),
        ],
        compiler_params=pltpu.CompilerParams(
            dimension_semantics=("arbitrary",),
            vmem_limit_bytes=VMEM_LIMIT_BYTES),
        name="mix_layer",
    )(xf, *small, w_in, w_pool, w_out, w_up, w_down)


def _ffn_layer(xf, g_ffn, w_up_bf, w_down_bf, g_final, *, final_norm):
    m, d = xf.shape
    d_ff = w_up_bf.shape[1]
    tm, tf = FFN_TM, FFN_TF
    n_tiles, n_j = m // tm, d_ff // tf
    assert m % tm == 0 and d_ff % tf == 0 and tf % FFN_SUB == 0 and n_j >= 2
    tile = pl.BlockSpec((tm, d), lambda i, j: (i, 0))
    tile_ahead = pl.BlockSpec((tm, d), lambda i, j: (jnp.minimum(i + (j + 1) // n_j, n_tiles - 1), 0))
    return pl.pallas_call(
        functools.partial(_ffn_kernel, final_norm=final_norm),
        grid=(n_tiles, n_j),
        in_specs=[
            tile_ahead,
            _resident((1, d)),
            pl.BlockSpec(memory_space=pl.ANY),
            pl.BlockSpec(memory_space=pl.ANY),
            _resident((1, d)),
        ],
        out_specs=tile,
        out_shape=jax.ShapeDtypeStruct((m, d), _f32),
        scratch_shapes=[
            pltpu.VMEM((2, tm, d), _bf16),
            pltpu.VMEM((2, d, tf), _bf16),
            pltpu.VMEM((2, tf, d), _bf16),
            pltpu.SemaphoreType.DMA((2, 2)),
        ],
        compiler_params=pltpu.CompilerParams(
            dimension_semantics=("arbitrary", "arbitrary"),
            vmem_limit_bytes=VMEM_LIMIT_BYTES),
        name="ffn_layer",
    )(xf, g_ffn.reshape(1, d), w_up_bf, w_down_bf, g_final.reshape(1, d))


def kernel(x, g_mix, w_in, g_v, w_s, b_s, w_pool, pool_scale, w_out, g_ffn, w_up, w_down, g_final):
    bsz, seq_len, d = x.shape
    depth = g_mix.shape[0]
    xf = x.reshape(bsz * seq_len, d)
    for layer in range(depth):
        xf, w_up_bf, w_down_bf = _mix_layer(
            xf, g_mix[layer], w_in[layer], g_v[layer], w_s[layer], b_s[layer], w_pool[layer],
            pool_scale[layer], w_out[layer], w_up[layer], w_down[layer], seq_len=seq_len)
        xf = _ffn_layer(xf, g_ffn[layer], w_up_bf, w_down_bf, g_final, final_norm=(layer == depth - 1))
    return xf.reshape(bsz, seq_len, d)
```

```python
import functools

import jax
import jax.numpy as jnp
from jax import lax
from jax.experimental import pallas as pl
from jax.experimental.pallas import tpu as pltpu

CHUNK = 128
A_HEAD_DIM = 128
POOL_WINDOWS = (2, 4, 8, 16)
EPS = 1e-6

HALO = 16
POOL_ROWS = 64
BF16_SUBLANES = 16
MIX_TM = 256
FFN_TM = 512
FFN_TF = 2048
FFN_SUB = 512
VMEM_LIMIT_BYTES = 60 * 1024 * 1024

LOAD_ROWS = 256
LOAD_COLS = 1024
LOAD_SLOTS = 8

_bf16 = jnp.bfloat16
_f32 = jnp.float32


def _rms_scale(x):
    return lax.rsqrt(jnp.mean(x * x, axis=-1, keepdims=True) + EPS)


def _gelu_tanh(x):
    c = 0.7978845608028654
    t = jnp.tanh(x * ((x * x) * (c * 0.044715) + c))
    return x * (0.5 * t + 0.5)


def _dot(a, b):
    return jnp.dot(a, b, preferred_element_type=_f32)


class _WeightLoader:
    def __init__(self, col_blocks, stage, sem):
        self.pieces = [(src, dst, r, c) for src, dst, c in col_blocks for r in range(0, src.shape[0], LOAD_ROWS)]
        self.block_end = []
        for src, _, _ in col_blocks:
            self.block_end.append((self.block_end[-1] if self.block_end else 0) + src.shape[0] // LOAD_ROWS)
        self.stage, self.sem = stage, sem
        self.done = 0

    def _read(self, i):
        src, _, r, c = self.pieces[i]
        slot = i % LOAD_SLOTS
        return pltpu.make_async_copy(src.at[r:r + LOAD_ROWS, c:c + LOAD_COLS], self.stage.at[slot], self.sem.at[slot])

    def start(self):
        for i in range(min(LOAD_SLOTS - 1, len(self.pieces))):
            self._read(i).start()

    def need(self, n_blocks):
        while self.done < self.block_end[n_blocks - 1]:
            i = self.done
            if i + LOAD_SLOTS - 1 < len(self.pieces):
                self._read(i + LOAD_SLOTS - 1).start()
            self._read(i).wait()
            _, dst, r, c = self.pieces[i]
            dst[r:r + LOAD_ROWS, c:c + LOAD_COLS] = self.stage[i % LOAD_SLOTS].astype(_bf16)
            self.done += 1


def _mix_kernel(x_ref, gmix_ref, gv_ref, ws_ref, bst_ref, pscale_ref,
                win_hbm, wpool_hbm, wout_hbm, wup_ref, wdown_ref,
                o_ref, wup_bf_ref, wdown_bf_ref,
                win_ref, wpool_ref, wout_ref, zext_ref, mixed_ref,
                stage_ref, pool_stage_ref, load_sem, pool_sem, *, seq_len):
    tm = x_ref.shape[0]
    a_width = gv_ref.shape[1]
    n_heads = a_width // A_HEAD_DIM
    group_dim = wpool_ref.shape[1]
    b_width = zext_ref.shape[1]
    u_col, v_col, z_col = 0, a_width, 2 * a_width
    step = pl.program_id(0)
    seq_pos0 = (step * tm) % seq_len
    next_seq_pos0 = ((step + 1) * tm) % seq_len

    def tile_body(first_step):
        if first_step:
            zext_ref[0:HALO, :] = jnp.zeros((HALO, zext_ref.shape[1]), _f32)
            pool_copy = pltpu.make_async_copy(wpool_hbm, pool_stage_ref, pool_sem.at[0])
            pool_copy.start()
            in_blocks = [(win_hbm, win_ref, c)
                         for c0, width in ((z_col, b_width), (v_col, a_width), (u_col, a_width))
                         for c in range(c0, c0 + width, LOAD_COLS)]
            out_blocks = [(wout_hbm, wout_ref, c) for c in range(0, wout_ref.shape[1], LOAD_COLS)]
            loader = _WeightLoader(in_blocks + out_blocks, stage_ref, load_sem)
            loader.start()
        z_blocks, a_blocks = b_width // LOAD_COLS, a_width // LOAD_COLS

        def weights_ready(n_blocks):
            if first_step:
                loader.need(n_blocks)

        def convert_mlp_weights(k, n=4):
            for src, dst in ((wup_ref, wup_bf_ref), (wdown_ref, wdown_bf_ref)):
                cols = src.shape[1] // n
                dst[:, k * cols:(k + 1) * cols] = src[:, k * cols:(k + 1) * cols].astype(_bf16)

        x = x_ref[...]
        hb = (x * _rms_scale(x) * gmix_ref[...]).astype(_bf16)

        def in_proj(c0, c1):
            return _dot(hb, win_ref[:, c0:c1])

        def pool_group(g):
            win = POOL_WINDOWS[g]
            gs = slice(g * group_dim, (g + 1) * group_dim)
            for r0 in range(0, tm, POOL_ROWS):
                zb = zext_ref[r0:r0 + HALO + POOL_ROWS, gs]
                s = zb
                shift = 1
                while shift < win:
                    s = s + pltpu.roll(s, shift, axis=0)
                    shift *= 2
                if r0 + 1 >= win:
                    mean = s[HALO:] * (1.0 / win)
                else:
                    pos = seq_pos0 + r0 + lax.broadcasted_iota(jnp.int32, (POOL_ROWS, 1), 0)
                    mean = s[HALO:] / jnp.minimum(pos + 1, win).astype(_f32)
                mixed_ref[r0:r0 + POOL_ROWS, a_width + g * group_dim:a_width + (g + 1) * group_dim] = (
                    mean - zb[HALO:]).astype(_bf16)

        row = lax.broadcasted_iota(jnp.int32, (CHUNK, CHUNK), 0)
        col = lax.broadcasted_iota(jnp.int32, (CHUNK, CHUNK), 1)
        causal = row >= col

        def normed_heads(vpart, hd0):
            out = []
            for k in range(vpart.shape[1] // A_HEAD_DIM):
                vh = vpart[:, k * A_HEAD_DIM:(k + 1) * A_HEAD_DIM]
                cs = slice((hd0 + k) * A_HEAD_DIM, (hd0 + k + 1) * A_HEAD_DIM)
                out.append((vh * _rms_scale(vh) * gv_ref[:, cs]).astype(_bf16))
            return out

        def gate_heads(upart, vns, hd0):
            for k, vn in enumerate(vns):
                hd = hd0 + k
                w = jnp.where(causal, ws_ref[hd], 0.0).astype(_bf16)
                bias = bst_ref[:, hd:hd + 1]
                n_chunks = tm // CHUNK
                vn_wide = jnp.concatenate([vn[c * CHUNK:(c + 1) * CHUNK] for c in range(n_chunks)], axis=1)
                mixed_wide = _dot(w, vn_wide) + bias
                for c in range(n_chunks):
                    rs = slice(c * CHUNK, (c + 1) * CHUNK)
                    uh = upart[rs, k * A_HEAD_DIM:(k + 1) * A_HEAD_DIM]
                    mixed_ref[rs, hd * A_HEAD_DIM:(hd + 1) * A_HEAD_DIM] = (
                        uh * mixed_wide[:, c * A_HEAD_DIM:(c + 1) * A_HEAD_DIM]).astype(_bf16)

        half_a, half_b, half_h, half_g = a_width // 2, zext_ref.shape[1] // 2, n_heads // 2, len(POOL_WINDOWS) // 2
        weights_ready(z_blocks)
        if first_step:
            pool_copy.wait()
            wpool_ref[...] = pool_stage_ref[...].astype(_bf16)
            for g in range(len(POOL_WINDOWS)):
                gs = slice(g * group_dim, (g + 1) * group_dim)
                zs = slice(z_col + g * group_dim, z_col + (g + 1) * group_dim)
                win_ref[:, zs] = (_dot(win_ref[:, zs], wpool_ref[g]) * pscale_ref[:, gs]).astype(_bf16)
        zext_ref[HALO:HALO + tm, 0:half_b] = in_proj(z_col, z_col + half_b)
        zext_ref[HALO:HALO + tm, half_b:] = in_proj(z_col + half_b, z_col + 2 * half_b)
        for g in range(half_g):
            pool_group(g)
        convert_mlp_weights(0)
        weights_ready(z_blocks + a_blocks)
        v0 = _gelu_tanh(in_proj(v_col, v_col + half_a))
        convert_mlp_weights(1)
        for g in range(half_g, 2 * half_g):
            pool_group(g)
        zext_ref[0:HALO, :] = jnp.where(next_seq_pos0 == 0, 0.0, zext_ref[tm:tm + HALO, :])
        v1 = _gelu_tanh(in_proj(v_col + half_a, v_col + 2 * half_a))
        convert_mlp_weights(2)
        vn0 = normed_heads(v0, 0)
        weights_ready(z_blocks + 2 * a_blocks)
        u0 = _gelu_tanh(in_proj(u_col, u_col + half_a))
        convert_mlp_weights(3)
        vn1 = normed_heads(v1, half_h)
        gate_heads(u0, vn0, 0)
        u1 = _gelu_tanh(in_proj(u_col + half_a, u_col + 2 * half_a))
        gate_heads(u1, vn1, half_h)

        weights_ready(z_blocks + 2 * a_blocks + wout_ref.shape[1] // LOAD_COLS)
        o_ref[...] = x + _dot(mixed_ref[...], wout_ref[...])

    pl.when(step == 0)(functools.partial(tile_body, True))
    pl.when(step > 0)(functools.partial(tile_body, False))


def _ffn_kernel(x_ref, gffn_ref, wup_ref, wdown_ref, gfin_ref, o_ref, h_ref, *, final_norm):
    i, j = pl.program_id(0), pl.program_id(1)
    n_j = pl.num_programs(1)
    n_sub = wup_ref.shape[1] // FFN_SUB
    h_cur, h_next = h_ref.at[i % 2], h_ref.at[(i + 1) % 2]

    tm = x_ref.shape[0]

    def normalise_input(dst, r0=0, r1=tm):
        x = x_ref[r0:r1, :]
        dst[r0:r1, :] = (x * _rms_scale(x) * gffn_ref[...]).astype(_bf16)

    @pl.when((i == 0) & (j == 0))
    def _():
        normalise_input(h_cur)

    def step(first, last):
        def up(k):
            cols = slice(k * FFN_SUB, (k + 1) * FFN_SUB)
            return jnp.square(jnp.maximum(_dot(h_cur[...], wup_ref[:, cols]), 0.0)).astype(_bf16)

        def down(k, act):
            d = _dot(act, wdown_ref[k * FFN_SUB:(k + 1) * FFN_SUB, :])
            if first and k == 0:
                o_ref[...] = x_ref[...] + d
            elif last and k == n_sub - 1 and final_norm:
                y = o_ref[...] + d
                o_ref[...] = y * _rms_scale(y) * gfin_ref[...]
            else:
                o_ref[...] += d

        act = up(0)
        for k in range(1, n_sub):
            if last:
                rows = tm // (n_sub - 1) // BF16_SUBLANES * BF16_SUBLANES
                normalise_input(h_next, (k - 1) * rows, tm if k == n_sub - 1 else k * rows)
            nxt = up(k)
            down(k - 1, act)
            act = nxt
        down(n_sub - 1, act)

    pl.when(j == 0)(functools.partial(step, True, False))
    pl.when((j > 0) & (j < n_j - 1))(functools.partial(step, False, False))
    pl.when(j == n_j - 1)(functools.partial(step, False, True))


def _resident(shape):
    zeros = (0,) * len(shape)
    return pl.BlockSpec(shape, lambda *_: zeros, pipeline_mode=pl.Buffered(1))


def _row_slab(w, n_steps):
    rows = w.shape[0] // n_steps
    assert w.shape[0] % n_steps == 0 and rows % BF16_SUBLANES == 0
    return pl.BlockSpec((rows, w.shape[1]), lambda i: (i, 0))


def _mix_layer(xf, g_mix, w_in, g_v, w_s, b_s, w_pool, pool_scale, w_out, w_up, w_down, *, seq_len):
    m, d = xf.shape
    a_width = g_v.shape[0]
    b_width = pool_scale.shape[0]
    tm = MIX_TM
    n_steps = m // tm
    assert m % tm == 0 and seq_len % tm == 0 and tm % CHUNK == 0 and tm % POOL_ROWS == 0 and tm >= HALO
    for w in (w_in, w_out):
        assert w.shape[0] % LOAD_ROWS == 0 and w.shape[1] % LOAD_COLS == 0
    assert a_width % LOAD_COLS == 0 and b_width % LOAD_COLS == 0
    tile = pl.BlockSpec((tm, d), lambda i: (i, 0))
    hbm = pl.BlockSpec(memory_space=pl.ANY)
    small = (
        g_mix.reshape(1, d),
        g_v.reshape(1, a_width),
        w_s,
        b_s.T,
        pool_scale.reshape(1, b_width),
    )
    slabs = [_row_slab(w_up, n_steps), _row_slab(w_down, n_steps)]
    return pl.pallas_call(
        functools.partial(_mix_kernel, seq_len=seq_len),
        grid=(n_steps,),
        in_specs=[tile] + [_resident(op.shape) for op in small] + [hbm, hbm, hbm] + slabs,
        out_specs=[tile] + slabs,
        out_shape=[jax.ShapeDtypeStruct((m, d), _f32),
                   jax.ShapeDtypeStruct(w_up.shape, _bf16),
                   jax.ShapeDtypeStruct(w_down.shape, _bf16)],
        scratch_shapes=[
            pltpu.VMEM(w_in.shape, _bf16),
            pltpu.VMEM(w_pool.shape, _bf16),
            pltpu.VMEM(w_out.shape, _bf16),
            pltpu.VMEM((HALO + tm, b_width), _f32),
            pltpu.VMEM((tm, a_width + b_width), _bf16),
            pltpu.VMEM((LOAD_SLOTS, LOAD_ROWS, LOAD_COLS), _f32),
            pltpu.VMEM(w_pool.shape, _f32),
            pltpu.SemaphoreType.DMA((LOAD_SLOTS,)),
            pltpu.SemaphoreType.DMA((1,)),
        ],
        compiler_params=pltpu.CompilerParams(
            dimension_semantics=("arbitrary",),
            vmem_limit_bytes=VMEM_LIMIT_BYTES),
        name="mix_layer",
    )(xf, *small, w_in, w_pool, w_out, w_up, w_down)


def _ffn_layer(xf, g_ffn, w_up_bf, w_down_bf, g_final, *, final_norm):
    m, d = xf.shape
    d_ff = w_up_bf.shape[1]
    tm, tf = FFN_TM, FFN_TF
    n_tiles, n_j = m // tm, d_ff // tf
    assert m % tm == 0 and d_ff % tf == 0 and tf % FFN_SUB == 0 and n_j >= 2
    tile = pl.BlockSpec((tm, d), lambda i, j: (i, 0))
    tile_ahead = pl.BlockSpec((tm, d), lambda i, j: (jnp.minimum(i + (j + 1) // n_j, n_tiles - 1), 0))
    return pl.pallas_call(
        functools.partial(_ffn_kernel, final_norm=final_norm),
        grid=(n_tiles, n_j),
        in_specs=[
            tile_ahead,
            _resident((1, d)),
            pl.BlockSpec((d, tf), lambda i, j: (0, j)),
            pl.BlockSpec((tf, d), lambda i, j: (j, 0)),
            _resident((1, d)),
        ],
        out_specs=tile,
        out_shape=jax.ShapeDtypeStruct((m, d), _f32),
        scratch_shapes=[
            pltpu.VMEM((2, tm, d), _bf16),
        ],
        compiler_params=pltpu.CompilerParams(
            dimension_semantics=("arbitrary", "arbitrary"),
            vmem_limit_bytes=VMEM_LIMIT_BYTES),
        name="ffn_layer",
    )(xf, g_ffn.reshape(1, d), w_up_bf, w_down_bf, g_final.reshape(1, d))


def kernel(x, g_mix, w_in, g_v, w_s, b_s, w_pool, pool_scale, w_out, g_ffn, w_up, w_down, g_final):
    bsz, seq_len, d = x.shape
    depth = g_mix.shape[0]
    xf = x.reshape(bsz * seq_len, d)
    for layer in range(depth):
        xf, w_up_bf, w_down_bf = _mix_layer(
            xf, g_mix[layer], w_in[layer], g_v[layer], w_s[layer], b_s[layer], w_pool[layer],
            pool_scale[layer], w_out[layer], w_up[layer], w_down[layer], seq_len=seq_len)
        xf = _ffn_layer(xf, g_ffn[layer], w_up_bf, w_down_bf, g_final, final_norm=(layer == depth - 1))
    return xf.reshape(bsz, seq_len, d)
```

```python
import functools

import jax
import jax.numpy as jnp
from jax import lax
from jax.experimental import pallas as pl
from jax.experimental.pallas import tpu as pltpu

CHUNK = 128
A_HEAD_DIM = 128
POOL_WINDOWS = (2, 4, 8, 16)
EPS = 1e-6

HALO = 16
POOL_ROWS = 64
BF16_SUBLANES = 16
MIX_TM = 256
FFN_TM = 512
FFN_TF = 2048
FFN_SUB = 512
VMEM_LIMIT_BYTES = 60 * 1024 * 1024
MIX_VMEM_LIMIT_BYTES = 63 * 1024 * 1024

LOAD_ROWS = 256
LOAD_COLS = 1024
LOAD_SLOTS = 8

_bf16 = jnp.bfloat16
_f32 = jnp.float32


def _rms_scale(x):
    return lax.rsqrt(jnp.mean(x * x, axis=-1, keepdims=True) + EPS)


def _gelu_tanh(x):
    c = 0.7978845608028654
    t = jnp.tanh(x * ((x * x) * (c * 0.044715) + c))
    return x * (0.5 * t + 0.5)


def _dot(a, b):
    return jnp.dot(a, b, preferred_element_type=_f32)


class _WeightLoader:
    def __init__(self, col_blocks, stage, sem):
        self.pieces = [(src, dst, r, c) for src, dst, c in col_blocks for r in range(0, src.shape[0], LOAD_ROWS)]
        self.block_end = []
        for src, _, _ in col_blocks:
            self.block_end.append((self.block_end[-1] if self.block_end else 0) + src.shape[0] // LOAD_ROWS)
        self.stage, self.sem = stage, sem
        self.done = 0

    def _read(self, i):
        src, _, r, c = self.pieces[i]
        slot = i % LOAD_SLOTS
        return pltpu.make_async_copy(src.at[r:r + LOAD_ROWS, c:c + LOAD_COLS], self.stage.at[slot], self.sem.at[slot])

    def start(self):
        for i in range(min(LOAD_SLOTS - 1, len(self.pieces))):
            self._read(i).start()

    def need(self, n_blocks):
        while self.done < self.block_end[n_blocks - 1]:
            i = self.done
            if i + LOAD_SLOTS - 1 < len(self.pieces):
                self._read(i + LOAD_SLOTS - 1).start()
            self._read(i).wait()
            _, dst, r, c = self.pieces[i]
            dst[r:r + LOAD_ROWS, c:c + LOAD_COLS] = self.stage[i % LOAD_SLOTS].astype(_bf16)
            self.done += 1


def _mix_kernel(x_ref, xprev_ref, gmix_ref, gv_ref, ws_ref, bst_ref, pscale_ref,
                win_hbm, wpool_hbm, wout_hbm, wup_ref, wdown_ref,
                o_ref, wup_bf_ref, wdown_bf_ref,
                win_ref, wpool_ref, wout_ref, zext_ref, pooled_ref, mixed2_ref,
                stage_ref, pool_stage_ref, load_sem, pool_sem, *, seq_len):
    tm = x_ref.shape[0]
    a_width = gv_ref.shape[1]
    n_heads = a_width // A_HEAD_DIM
    group_dim = wpool_ref.shape[1]
    b_width = zext_ref.shape[1]
    u_col, v_col, z_col = 0, a_width, 2 * a_width
    step = pl.program_id(0)
    seq_pos0 = (step * tm) % seq_len
    next_seq_pos0 = ((step + 1) * tm) % seq_len

    mixed_ref = mixed2_ref.at[step % 2]
    mixed_prev_ref = mixed2_ref.at[(step + 1) % 2]

    def finish_previous_tile():
        o_ref[...] = xprev_ref[...] + _dot(mixed_prev_ref[...], wout_ref[...])

    def tile_body(first_step):
        if not first_step:
            finish_previous_tile()
        if first_step:
            zext_ref[0:HALO, :] = jnp.zeros((HALO, zext_ref.shape[1]), _f32)
            pool_copy = pltpu.make_async_copy(wpool_hbm, pool_stage_ref, pool_sem.at[0])
            pool_copy.start()
            in_blocks = [(win_hbm, win_ref, c)
                         for c0, width in ((z_col, b_width), (v_col, a_width), (u_col, a_width))
                         for c in range(c0, c0 + width, LOAD_COLS)]
            out_blocks = [(wout_hbm, wout_ref, c) for c in range(0, wout_ref.shape[1], LOAD_COLS)]
            loader = _WeightLoader(in_blocks + out_blocks, stage_ref, load_sem)
            loader.start()
        z_blocks, a_blocks = b_width // LOAD_COLS, a_width // LOAD_COLS

        def weights_ready(n_blocks):
            if first_step:
                loader.need(n_blocks)

        def convert_mlp_weights(k, n=4):
            for src, dst in ((wup_ref, wup_bf_ref), (wdown_ref, wdown_bf_ref)):
                cols = src.shape[1] // n
                dst[:, k * cols:(k + 1) * cols] = src[:, k * cols:(k + 1) * cols].astype(_bf16)

        x = x_ref[...]
        hb = (x * _rms_scale(x) * gmix_ref[...]).astype(_bf16)

        def in_proj(c0, c1):
            return _dot(hb, win_ref[:, c0:c1])

        def pool_group(g):
            win = POOL_WINDOWS[g]
            gs = slice(g * group_dim, (g + 1) * group_dim)
            for r0 in range(0, tm, POOL_ROWS):
                zb = zext_ref[r0:r0 + HALO + POOL_ROWS, gs]
                s = zb
                shift = 1
                while shift < win:
                    s = s + pltpu.roll(s, shift, axis=0)
                    shift *= 2
                if r0 + 1 >= win:
                    mean = s[HALO:] * (1.0 / win)
                else:
                    pos = seq_pos0 + r0 + lax.broadcasted_iota(jnp.int32, (POOL_ROWS, 1), 0)
                    mean = s[HALO:] / jnp.minimum(pos + 1, win).astype(_f32)
                pooled_ref[r0:r0 + POOL_ROWS, gs] = (mean - zb[HALO:]).astype(_bf16)

        def pool_project(g):
            gs = slice(g * group_dim, (g + 1) * group_dim)
            y = _dot(pooled_ref[:, gs], wpool_ref[g]) * pscale_ref[:, gs]
            mixed_ref[:, a_width + g * group_dim:a_width + (g + 1) * group_dim] = y.astype(_bf16)

        row = lax.broadcasted_iota(jnp.int32, (CHUNK, CHUNK), 0)
        col = lax.broadcasted_iota(jnp.int32, (CHUNK, CHUNK), 1)
        causal = row >= col

        def normed_heads(vpart, hd0):
            out = []
            for k in range(vpart.shape[1] // A_HEAD_DIM):
                vh = vpart[:, k * A_HEAD_DIM:(k + 1) * A_HEAD_DIM]
                cs = slice((hd0 + k) * A_HEAD_DIM, (hd0 + k + 1) * A_HEAD_DIM)
                out.append((vh * _rms_scale(vh) * gv_ref[:, cs]).astype(_bf16))
            return out

        def gate_heads(upart, vns, hd0):
            for k, vn in enumerate(vns):
                hd = hd0 + k
                w = jnp.where(causal, ws_ref[hd], 0.0).astype(_bf16)
                bias = bst_ref[:, hd:hd + 1]
                n_chunks = tm // CHUNK
                vn_wide = jnp.concatenate([vn[c * CHUNK:(c + 1) * CHUNK] for c in range(n_chunks)], axis=1)
                mixed_wide = _dot(w, vn_wide) + bias
                for c in range(n_chunks):
                    rs = slice(c * CHUNK, (c + 1) * CHUNK)
                    uh = upart[rs, k * A_HEAD_DIM:(k + 1) * A_HEAD_DIM]
                    mixed_ref[rs, hd * A_HEAD_DIM:(hd + 1) * A_HEAD_DIM] = (
                        uh * mixed_wide[:, c * A_HEAD_DIM:(c + 1) * A_HEAD_DIM]).astype(_bf16)

        half_a, half_b, half_h, half_g = a_width // 2, zext_ref.shape[1] // 2, n_heads // 2, len(POOL_WINDOWS) // 2
        weights_ready(z_blocks)
        zext_ref[HALO:HALO + tm, 0:half_b] = in_proj(z_col, z_col + half_b)
        zext_ref[HALO:HALO + tm, half_b:] = in_proj(z_col + half_b, z_col + 2 * half_b)
        for g in range(half_g):
            pool_group(g)
        if first_step:
            pool_copy.wait()
            wpool_ref[...] = pool_stage_ref[...].astype(_bf16)
        for g in range(half_g):
            pool_project(g)
        convert_mlp_weights(0)
        weights_ready(z_blocks + a_blocks)
        v0 = _gelu_tanh(in_proj(v_col, v_col + half_a))
        convert_mlp_weights(1)
        for g in range(half_g, 2 * half_g):
            pool_group(g)
        zext_ref[0:HALO, :] = jnp.where(next_seq_pos0 == 0, 0.0, zext_ref[tm:tm + HALO, :])
        for g in range(half_g, 2 * half_g):
            pool_project(g)
        v1 = _gelu_tanh(in_proj(v_col + half_a, v_col + 2 * half_a))
        convert_mlp_weights(2)
        vn0 = normed_heads(v0, 0)
        weights_ready(z_blocks + 2 * a_blocks)
        u0 = _gelu_tanh(in_proj(u_col, u_col + half_a))
        convert_mlp_weights(3)
        vn1 = normed_heads(v1, half_h)
        gate_heads(u0, vn0, 0)
        u1 = _gelu_tanh(in_proj(u_col + half_a, u_col + 2 * half_a))
        gate_heads(u1, vn1, half_h)

        weights_ready(z_blocks + 2 * a_blocks + wout_ref.shape[1] // LOAD_COLS)

    n_tiles = pl.num_programs(0) - 1
    pl.when(step == 0)(functools.partial(tile_body, True))
    pl.when((step > 0) & (step < n_tiles))(functools.partial(tile_body, False))
    pl.when(step == n_tiles)(finish_previous_tile)


def _ffn_kernel(x_ref, gffn_ref, wup_ref, wdown_ref, gfin_ref, o_ref, h_ref, *, final_norm):
    i, j = pl.program_id(0), pl.program_id(1)
    n_j = pl.num_programs(1)
    n_sub = wup_ref.shape[1] // FFN_SUB
    h_cur, h_next = h_ref.at[i % 2], h_ref.at[(i + 1) % 2]

    tm = x_ref.shape[0]

    def normalise_input(dst, r0=0, r1=tm):
        x = x_ref[r0:r1, :]
        dst[r0:r1, :] = (x * _rms_scale(x) * gffn_ref[...]).astype(_bf16)

    @pl.when((i == 0) & (j == 0))
    def _():
        normalise_input(h_cur)

    def step(first, last):
        def up(k):
            cols = slice(k * FFN_SUB, (k + 1) * FFN_SUB)
            return jnp.square(jnp.maximum(_dot(h_cur[...], wup_ref[:, cols]), 0.0)).astype(_bf16)

        def down(k, act):
            d = _dot(act, wdown_ref[k * FFN_SUB:(k + 1) * FFN_SUB, :])
            if first and k == 0:
                o_ref[...] = x_ref[...] + d
            elif last and k == n_sub - 1 and final_norm:
                y = o_ref[...] + d
                o_ref[...] = y * _rms_scale(y) * gfin_ref[...]
            else:
                o_ref[...] += d

        act = up(0)
        for k in range(1, n_sub):
            if last:
                rows = tm // (n_sub - 1) // BF16_SUBLANES * BF16_SUBLANES
                normalise_input(h_next, (k - 1) * rows, tm if k == n_sub - 1 else k * rows)
            nxt = up(k)
            down(k - 1, act)
            act = nxt
        down(n_sub - 1, act)

    pl.when(j == 0)(functools.partial(step, True, False))
    pl.when((j > 0) & (j < n_j - 1))(functools.partial(step, False, False))
    pl.when(j == n_j - 1)(functools.partial(step, False, True))


def _resident(shape):
    zeros = (0,) * len(shape)
    return pl.BlockSpec(shape, lambda *_: zeros, pipeline_mode=pl.Buffered(1))


def _row_slab(w, n_steps):
    rows = w.shape[0] // n_steps
    assert w.shape[0] % n_steps == 0 and rows % BF16_SUBLANES == 0
    return pl.BlockSpec((rows, w.shape[1]), lambda i: (jnp.minimum(i, n_steps - 1), 0))


def _mix_layer(xf, g_mix, w_in, g_v, w_s, b_s, w_pool, pool_scale, w_out, w_up, w_down, *, seq_len):
    m, d = xf.shape
    a_width = g_v.shape[0]
    b_width = pool_scale.shape[0]
    tm = MIX_TM
    n_steps = m // tm
    assert m % tm == 0 and seq_len % tm == 0 and tm % CHUNK == 0 and tm % POOL_ROWS == 0 and tm >= HALO
    for w in (w_in, w_out):
        assert w.shape[0] % LOAD_ROWS == 0 and w.shape[1] % LOAD_COLS == 0
    assert a_width % LOAD_COLS == 0 and b_width % LOAD_COLS == 0
    tile = pl.BlockSpec((tm, d), lambda i: (jnp.minimum(i, n_steps - 1), 0))
    tile_behind = pl.BlockSpec((tm, d), lambda i: (jnp.maximum(i - 1, 0), 0))
    hbm = pl.BlockSpec(memory_space=pl.ANY)
    small = (
        g_mix.reshape(1, d),
        g_v.reshape(1, a_width),
        w_s,
        b_s.T,
        pool_scale.reshape(1, b_width),
    )
    slabs = [_row_slab(w_up, n_steps), _row_slab(w_down, n_steps)]
    return pl.pallas_call(
        functools.partial(_mix_kernel, seq_len=seq_len),
        grid=(n_steps + 1,),
        in_specs=[tile, tile_behind] + [_resident(op.shape) for op in small] + [hbm, hbm, hbm] + slabs,
        out_specs=[tile_behind] + slabs,
        out_shape=[jax.ShapeDtypeStruct((m, d), _f32),
                   jax.ShapeDtypeStruct(w_up.shape, _bf16),
                   jax.ShapeDtypeStruct(w_down.shape, _bf16)],
        scratch_shapes=[
            pltpu.VMEM(w_in.shape, _bf16),
            pltpu.VMEM(w_pool.shape, _bf16),
            pltpu.VMEM(w_out.shape, _bf16),
            pltpu.VMEM((HALO + tm, b_width), _f32),
            pltpu.VMEM((tm, b_width), _bf16),
            pltpu.VMEM((2, tm, a_width + b_width), _bf16),
            pltpu.VMEM((LOAD_SLOTS, LOAD_ROWS, LOAD_COLS), _f32),
            pltpu.VMEM(w_pool.shape, _f32),
            pltpu.SemaphoreType.DMA((LOAD_SLOTS,)),
            pltpu.SemaphoreType.DMA((1,)),
        ],
        compiler_params=pltpu.CompilerParams(
            dimension_semantics=("arbitrary",),
            vmem_limit_bytes=MIX_VMEM_LIMIT_BYTES),
        name="mix_layer",
    )(xf, xf, *small, w_in, w_pool, w_out, w_up, w_down)


def _ffn_layer(xf, g_ffn, w_up_bf, w_down_bf, g_final, *, final_norm):
    m, d = xf.shape
    d_ff = w_up_bf.shape[1]
    tm, tf = FFN_TM, FFN_TF
    n_tiles, n_j = m // tm, d_ff // tf
    assert m % tm == 0 and d_ff % tf == 0 and tf % FFN_SUB == 0 and n_j >= 2
    tile = pl.BlockSpec((tm, d), lambda i, j: (i, 0))
    tile_ahead = pl.BlockSpec((tm, d), lambda i, j: (jnp.minimum(i + (j + 1) // n_j, n_tiles - 1), 0))
    return pl.pallas_call(
        functools.partial(_ffn_kernel, final_norm=final_norm),
        grid=(n_tiles, n_j),
        in_specs=[
            tile_ahead,
            _resident((1, d)),
            pl.BlockSpec((d, tf), lambda i, j: (0, j)),
            pl.BlockSpec((tf, d), lambda i, j: (j, 0)),
            _resident((1, d)),
        ],
        out_specs=tile,
        out_shape=jax.ShapeDtypeStruct((m, d), _f32),
        scratch_shapes=[
            pltpu.VMEM((2, tm, d), _bf16),
        ],
        compiler_params=pltpu.CompilerParams(
            dimension_semantics=("arbitrary", "arbitrary"),
            vmem_limit_bytes=VMEM_LIMIT_BYTES),
        name="ffn_layer",
    )(xf, g_ffn.reshape(1, d), w_up_bf, w_down_bf, g_final.reshape(1, d))


def kernel(x, g_mix, w_in, g_v, w_s, b_s, w_pool, pool_scale, w_out, g_ffn, w_up, w_down, g_final):
    bsz, seq_len, d = x.shape
    depth = g_mix.shape[0]
    xf = x.reshape(bsz * seq_len, d)
    for layer in range(depth):
        xf, w_up_bf, w_down_bf = _mix_layer(
            xf, g_mix[layer], w_in[layer], g_v[layer], w_s[layer], b_s[layer], w_pool[layer],
            pool_scale[layer], w_out[layer], w_up[layer], w_down[layer], seq_len=seq_len)
        xf = _ffn_layer(xf, g_ffn[layer], w_up_bf, w_down_bf, g_final, final_norm=(layer == depth - 1))
    return xf.reshape(bsz, seq_len, d)
```

```python
import functools

import jax
import jax.numpy as jnp
from jax import lax
from jax.experimental import pallas as pl
from jax.experimental.pallas import tpu as pltpu

CHUNK = 128
A_HEAD_DIM = 128
POOL_WINDOWS = (2, 4, 8, 16)
EPS = 1e-6

HALO = 16
POOL_ROWS = 64
BF16_SUBLANES = 16
MIX_TM = 256
FFN_TM = 512
FFN_TF = 2048
FFN_SUB = 512
VMEM_LIMIT_BYTES = 60 * 1024 * 1024

LOAD_ROWS = 256
LOAD_COLS = 1024
LOAD_SLOTS = 8

_bf16 = jnp.bfloat16
_f32 = jnp.float32


def _rms_scale(x):
    return lax.rsqrt(jnp.mean(x * x, axis=-1, keepdims=True) + EPS)


def _gelu_tanh(x):
    c = 0.7978845608028654
    t = jnp.tanh(x * ((x * x) * (c * 0.044715) + c))
    return x * (0.5 * t + 0.5)


def _dot(a, b):
    return jnp.dot(a, b, preferred_element_type=_f32)


class _WeightLoader:
    def __init__(self, col_blocks, stage, sem, convert=None):
        self.convert = convert or (lambda src, c, piece: piece.astype(_bf16))
        self.pieces = [(src, dst, r, c) for src, dst, c in col_blocks for r in range(0, src.shape[0], LOAD_ROWS)]
        self.block_end = []
        for src, _, _ in col_blocks:
            self.block_end.append((self.block_end[-1] if self.block_end else 0) + src.shape[0] // LOAD_ROWS)
        self.stage, self.sem = stage, sem
        self.done = 0

    def _read(self, i):
        src, _, r, c = self.pieces[i]
        slot = i % LOAD_SLOTS
        return pltpu.make_async_copy(src.at[r:r + LOAD_ROWS, c:c + LOAD_COLS], self.stage.at[slot], self.sem.at[slot])

    def start(self):
        for i in range(min(LOAD_SLOTS - 1, len(self.pieces))):
            self._read(i).start()

    def need(self, n_blocks):
        while self.done < self.block_end[n_blocks - 1]:
            i = self.done
            if i + LOAD_SLOTS - 1 < len(self.pieces):
                self._read(i + LOAD_SLOTS - 1).start()
            self._read(i).wait()
            src, dst, r, c = self.pieces[i]
            dst[r:r + LOAD_ROWS, c:c + LOAD_COLS] = self.convert(src, c, self.stage[i % LOAD_SLOTS])
            self.done += 1


def _mix_kernel(x_ref, gmix_ref, gv_ref, ws_ref, bst_ref, pscale_ref,
                win_hbm, wpool_hbm, wout_hbm, wup_ref, wdown_ref,
                o_ref, wup_bf_ref, wdown_bf_ref,
                win_ref, wpool_ref, wout_ref, zext_ref, mixed_ref,
                stage_ref, pool_stage_ref, load_sem, pool_sem, *, seq_len):
    tm = x_ref.shape[0]
    a_width = gv_ref.shape[1]
    n_heads = a_width // A_HEAD_DIM
    group_dim = wpool_ref.shape[1]
    b_width = zext_ref.shape[1]
    u_col, v_col, z_col = 0, a_width, 2 * a_width
    step = pl.program_id(0)
    seq_pos0 = (step * tm) % seq_len
    next_seq_pos0 = ((step + 1) * tm) % seq_len

    def tile_body(first_step):
        if first_step:
            zext_ref[0:HALO, :] = jnp.zeros((HALO, zext_ref.shape[1]), _f32)
            pool_copy = pltpu.make_async_copy(wpool_hbm, pool_stage_ref, pool_sem.at[0])
            pool_copy.start()
            in_blocks = [(win_hbm, win_ref, c)
                         for c0, width in ((z_col, b_width), (v_col, a_width), (u_col, a_width))
                         for c in range(c0, c0 + width, LOAD_COLS)]
            out_blocks = [(wout_hbm, wout_ref, c) for c in range(0, wout_ref.shape[1], LOAD_COLS)]

            def convert(src, c, piece):
                if src is not win_hbm or not z_col <= c < z_col + b_width:
                    return piece.astype(_bf16)
                parts = []
                for k in range(LOAD_COLS // group_dim):
                    g = (c - z_col) // group_dim + k
                    gs = slice(g * group_dim, (g + 1) * group_dim)
                    wz = piece[:, k * group_dim:(k + 1) * group_dim].astype(_bf16)
                    parts.append((_dot(wz, wpool_ref[g]) * pscale_ref[:, gs]).astype(_bf16))
                return jnp.concatenate(parts, axis=1)

            loader = _WeightLoader(in_blocks + out_blocks, stage_ref, load_sem, convert)
            loader.start()
            pool_copy.wait()
            wpool_ref[...] = pool_stage_ref[...].astype(_bf16)
        z_blocks, a_blocks = b_width // LOAD_COLS, a_width // LOAD_COLS

        def weights_ready(n_blocks):
            if first_step:
                loader.need(n_blocks)

        def convert_mlp_weights(k, n=4):
            for src, dst in ((wup_ref, wup_bf_ref), (wdown_ref, wdown_bf_ref)):
                cols = src.shape[1] // n
                dst[:, k * cols:(k + 1) * cols] = src[:, k * cols:(k + 1) * cols].astype(_bf16)

        x = x_ref[...]
        hb = (x * _rms_scale(x) * gmix_ref[...]).astype(_bf16)

        def in_proj(c0, c1):
            return _dot(hb, win_ref[:, c0:c1])

        def pool_group(g):
            win = POOL_WINDOWS[g]
            gs = slice(g * group_dim, (g + 1) * group_dim)
            for r0 in range(0, tm, POOL_ROWS):
                zb = zext_ref[r0:r0 + HALO + POOL_ROWS, gs]
                s = zb
                shift = 1
                while shift < win:
                    s = s + pltpu.roll(s, shift, axis=0)
                    shift *= 2
                if r0 + 1 >= win:
                    mean = s[HALO:] * (1.0 / win)
                else:
                    pos = seq_pos0 + r0 + lax.broadcasted_iota(jnp.int32, (POOL_ROWS, 1), 0)
                    mean = s[HALO:] / jnp.minimum(pos + 1, win).astype(_f32)
                mixed_ref[r0:r0 + POOL_ROWS, a_width + g * group_dim:a_width + (g + 1) * group_dim] = (
                    mean - zb[HALO:]).astype(_bf16)

        row = lax.broadcasted_iota(jnp.int32, (CHUNK, CHUNK), 0)
        col = lax.broadcasted_iota(jnp.int32, (CHUNK, CHUNK), 1)
        causal = row >= col

        def normed_heads(vpart, hd0):
            out = []
            for k in range(vpart.shape[1] // A_HEAD_DIM):
                vh = vpart[:, k * A_HEAD_DIM:(k + 1) * A_HEAD_DIM]
                cs = slice((hd0 + k) * A_HEAD_DIM, (hd0 + k + 1) * A_HEAD_DIM)
                out.append((vh * _rms_scale(vh) * gv_ref[:, cs]).astype(_bf16))
            return out

        def gate_heads(upart, vns, hd0):
            for k, vn in enumerate(vns):
                hd = hd0 + k
                w = jnp.where(causal, ws_ref[hd], 0.0).astype(_bf16)
                bias = bst_ref[:, hd:hd + 1]
                n_chunks = tm // CHUNK
                vn_wide = jnp.concatenate([vn[c * CHUNK:(c + 1) * CHUNK] for c in range(n_chunks)], axis=1)
                mixed_wide = _dot(w, vn_wide) + bias
                for c in range(n_chunks):
                    rs = slice(c * CHUNK, (c + 1) * CHUNK)
                    uh = upart[rs, k * A_HEAD_DIM:(k + 1) * A_HEAD_DIM]
                    mixed_ref[rs, hd * A_HEAD_DIM:(hd + 1) * A_HEAD_DIM] = (
                        uh * mixed_wide[:, c * A_HEAD_DIM:(c + 1) * A_HEAD_DIM]).astype(_bf16)

        half_a, half_b, half_h, half_g = a_width // 2, zext_ref.shape[1] // 2, n_heads // 2, len(POOL_WINDOWS) // 2
        weights_ready(z_blocks)
        zext_ref[HALO:HALO + tm, 0:half_b] = in_proj(z_col, z_col + half_b)
        zext_ref[HALO:HALO + tm, half_b:] = in_proj(z_col + half_b, z_col + 2 * half_b)
        for g in range(half_g):
            pool_group(g)
        convert_mlp_weights(0)
        weights_ready(z_blocks + a_blocks)
        v0 = _gelu_tanh(in_proj(v_col, v_col + half_a))
        convert_mlp_weights(1)
        for g in range(half_g, 2 * half_g):
            pool_group(g)
        zext_ref[0:HALO, :] = jnp.where(next_seq_pos0 == 0, 0.0, zext_ref[tm:tm + HALO, :])
        v1 = _gelu_tanh(in_proj(v_col + half_a, v_col + 2 * half_a))
        convert_mlp_weights(2)
        vn0 = normed_heads(v0, 0)
        weights_ready(z_blocks + 2 * a_blocks)
        u0 = _gelu_tanh(in_proj(u_col, u_col + half_a))
        convert_mlp_weights(3)
        vn1 = normed_heads(v1, half_h)
        gate_heads(u0, vn0, 0)
        u1 = _gelu_tanh(in_proj(u_col + half_a, u_col + 2 * half_a))
        gate_heads(u1, vn1, half_h)

        weights_ready(z_blocks + 2 * a_blocks + wout_ref.shape[1] // LOAD_COLS)
        o_ref[...] = x + _dot(mixed_ref[...], wout_ref[...])

    pl.when(step == 0)(functools.partial(tile_body, True))
    pl.when(step > 0)(functools.partial(tile_body, False))


def _ffn_kernel(x_ref, gffn_ref, wup_ref, wdown_ref, gfin_ref, o_ref, h_ref, *, final_norm):
    i, j = pl.program_id(0), pl.program_id(1)
    n_j = pl.num_programs(1)
    n_sub = wup_ref.shape[1] // FFN_SUB
    h_cur, h_next = h_ref.at[i % 2], h_ref.at[(i + 1) % 2]

    tm = x_ref.shape[0]

    def normalise_input(dst, r0=0, r1=tm):
        x = x_ref[r0:r1, :]
        dst[r0:r1, :] = (x * _rms_scale(x) * gffn_ref[...]).astype(_bf16)

    @pl.when((i == 0) & (j == 0))
    def _():
        normalise_input(h_cur)

    def step(first, last):
        def up(k):
            cols = slice(k * FFN_SUB, (k + 1) * FFN_SUB)
            return jnp.square(jnp.maximum(_dot(h_cur[...], wup_ref[:, cols]), 0.0)).astype(_bf16)

        def down(k, act):
            d = _dot(act, wdown_ref[k * FFN_SUB:(k + 1) * FFN_SUB, :])
            if first and k == 0:
                o_ref[...] = x_ref[...] + d
            elif last and k == n_sub - 1 and final_norm:
                y = o_ref[...] + d
                o_ref[...] = y * _rms_scale(y) * gfin_ref[...]
            else:
                o_ref[...] += d

        act = up(0)
        for k in range(1, n_sub):
            if last:
                rows = tm // (n_sub - 1) // BF16_SUBLANES * BF16_SUBLANES
                normalise_input(h_next, (k - 1) * rows, tm if k == n_sub - 1 else k * rows)
            nxt = up(k)
            down(k - 1, act)
            act = nxt
        down(n_sub - 1, act)

    pl.when(j == 0)(functools.partial(step, True, False))
    pl.when((j > 0) & (j < n_j - 1))(functools.partial(step, False, False))
    pl.when(j == n_j - 1)(functools.partial(step, False, True))


def _resident(shape):
    zeros = (0,) * len(shape)
    return pl.BlockSpec(shape, lambda *_: zeros, pipeline_mode=pl.Buffered(1))


def _row_slab(w, n_steps):
    rows = w.shape[0] // n_steps
    assert w.shape[0] % n_steps == 0 and rows % BF16_SUBLANES == 0
    return pl.BlockSpec((rows, w.shape[1]), lambda i: (i, 0))


def _mix_layer(xf, g_mix, w_in, g_v, w_s, b_s, w_pool, pool_scale, w_out, w_up, w_down, *, seq_len):
    m, d = xf.shape
    a_width = g_v.shape[0]
    b_width = pool_scale.shape[0]
    tm = MIX_TM
    n_steps = m // tm
    assert m % tm == 0 and seq_len % tm == 0 and tm % CHUNK == 0 and tm % POOL_ROWS == 0 and tm >= HALO
    for w in (w_in, w_out):
        assert w.shape[0] % LOAD_ROWS == 0 and w.shape[1] % LOAD_COLS == 0
    assert a_width % LOAD_COLS == 0 and b_width % LOAD_COLS == 0
    tile = pl.BlockSpec((tm, d), lambda i: (i, 0))
    hbm = pl.BlockSpec(memory_space=pl.ANY)
    small = (
        g_mix.reshape(1, d),
        g_v.reshape(1, a_width),
        w_s,
        b_s.T,
        pool_scale.reshape(1, b_width),
    )
    slabs = [_row_slab(w_up, n_steps), _row_slab(w_down, n_steps)]
    return pl.pallas_call(
        functools.partial(_mix_kernel, seq_len=seq_len),
        grid=(n_steps,),
        in_specs=[tile] + [_resident(op.shape) for op in small] + [hbm, hbm, hbm] + slabs,
        out_specs=[tile] + slabs,
        out_shape=[jax.ShapeDtypeStruct((m, d), _f32),
                   jax.ShapeDtypeStruct(w_up.shape, _bf16),
                   jax.ShapeDtypeStruct(w_down.shape, _bf16)],
        scratch_shapes=[
            pltpu.VMEM(w_in.shape, _bf16),
            pltpu.VMEM(w_pool.shape, _bf16),
            pltpu.VMEM(w_out.shape, _bf16),
            pltpu.VMEM((HALO + tm, b_width), _f32),
            pltpu.VMEM((tm, a_width + b_width), _bf16),
            pltpu.VMEM((LOAD_SLOTS, LOAD_ROWS, LOAD_COLS), _f32),
            pltpu.VMEM(w_pool.shape, _f32),
            pltpu.SemaphoreType.DMA((LOAD_SLOTS,)),
            pltpu.SemaphoreType.DMA((1,)),
        ],
        compiler_params=pltpu.CompilerParams(
            dimension_semantics=("arbitrary",),
            vmem_limit_bytes=VMEM_LIMIT_BYTES),
        name="mix_layer",
    )(xf, *small, w_in, w_pool, w_out, w_up, w_down)


def _ffn_layer(xf, g_ffn, w_up_bf, w_down_bf, g_final, *, final_norm):
    m, d = xf.shape
    d_ff = w_up_bf.shape[1]
    tm, tf = FFN_TM, FFN_TF
    n_tiles, n_j = m // tm, d_ff // tf
    assert m % tm == 0 and d_ff % tf == 0 and tf % FFN_SUB == 0 and n_j >= 2
    tile = pl.BlockSpec((tm, d), lambda i, j: (i, 0))
    tile_ahead = pl.BlockSpec((tm, d), lambda i, j: (jnp.minimum(i + (j + 1) // n_j, n_tiles - 1), 0))
    return pl.pallas_call(
        functools.partial(_ffn_kernel, final_norm=final_norm),
        grid=(n_tiles, n_j),
        in_specs=[
            tile_ahead,
            _resident((1, d)),
            pl.BlockSpec((d, tf), lambda i, j: (0, j)),
            pl.BlockSpec((tf, d), lambda i, j: (j, 0)),
            _resident((1, d)),
        ],
        out_specs=tile,
        out_shape=jax.ShapeDtypeStruct((m, d), _f32),
        scratch_shapes=[
            pltpu.VMEM((2, tm, d), _bf16),
        ],
        compiler_params=pltpu.CompilerParams(
            dimension_semantics=("arbitrary", "arbitrary"),
            vmem_limit_bytes=VMEM_LIMIT_BYTES),
        name="ffn_layer",
    )(xf, g_ffn.reshape(1, d), w_up_bf, w_down_bf, g_final.reshape(1, d))


def kernel(x, g_mix, w_in, g_v, w_s, b_s, w_pool, pool_scale, w_out, g_ffn, w_up, w_down, g_final):
    bsz, seq_len, d = x.shape
    depth = g_mix.shape[0]
    xf = x.reshape(bsz * seq_len, d)
    for layer in range(depth):
        xf, w_up_bf, w_down_bf = _mix_layer(
            xf, g_mix[layer], w_in[layer], g_v[layer], w_s[layer], b_s[layer], w_pool[layer],
            pool_scale[layer], w_out[layer], w_up[layer], w_down[layer], seq_len=seq_len)
        xf = _ffn_layer(xf, g_ffn[layer], w_up_bf, w_down_bf, g_final, final_norm=(layer == depth - 1))
    return xf.reshape(bsz, seq_len, d)
```

```python
import functools

import jax
import jax.numpy as jnp
from jax import lax
from jax.experimental import pallas as pl
from jax.experimental.pallas import tpu as pltpu

CHUNK = 128
A_HEAD_DIM = 128
POOL_WINDOWS = (2, 4, 8, 16)
EPS = 1e-6

HALO = 16
POOL_ROWS = 64
BF16_SUBLANES = 16
MIX_TM = 256
FFN_TM = 512
FFN_TF = 2048
FFN_SUB = 512
VMEM_LIMIT_BYTES = 60 * 1024 * 1024
MIX_VMEM_LIMIT_BYTES = 63 * 1024 * 1024

LOAD_ROWS = 256
LOAD_COLS = 1024
LOAD_SLOTS = 12

_bf16 = jnp.bfloat16
_f32 = jnp.float32


def _rms_scale(x):
    return lax.rsqrt(jnp.mean(x * x, axis=-1, keepdims=True) + EPS)


def _gelu_tanh(x):
    c = 0.7978845608028654
    t = jnp.tanh(x * ((x * x) * (c * 0.044715) + c))
    return x * (0.5 * t + 0.5)


def _dot(a, b):
    return jnp.dot(a, b, preferred_element_type=_f32)


class _WeightLoader:
    def __init__(self, col_blocks, stage, sem):
        self.pieces = [(src, dst, r, c) for src, dst, c in col_blocks for r in range(0, src.shape[0], LOAD_ROWS)]
        self.block_end = []
        for src, _, _ in col_blocks:
            self.block_end.append((self.block_end[-1] if self.block_end else 0) + src.shape[0] // LOAD_ROWS)
        self.stage, self.sem = stage, sem
        self.done = 0

    def _read(self, i):
        src, _, r, c = self.pieces[i]
        slot = i % LOAD_SLOTS
        return pltpu.make_async_copy(src.at[r:r + LOAD_ROWS, c:c + LOAD_COLS], self.stage.at[slot], self.sem.at[slot])

    def start(self):
        for i in range(min(LOAD_SLOTS - 1, len(self.pieces))):
            self._read(i).start()

    def need(self, n_blocks):
        while self.done < self.block_end[n_blocks - 1]:
            i = self.done
            if i + LOAD_SLOTS - 1 < len(self.pieces):
                self._read(i + LOAD_SLOTS - 1).start()
            self._read(i).wait()
            _, dst, r, c = self.pieces[i]
            dst[r:r + LOAD_ROWS, c:c + LOAD_COLS] = self.stage[i % LOAD_SLOTS].astype(_bf16)
            self.done += 1


def _mix_kernel(x_ref, gmix_ref, gv_ref, ws_ref, bst_ref, pscale_ref,
                win_hbm, wpool_hbm, wout_hbm, wup_ref, wdown_ref,
                o_ref, wup_bf_ref, wdown_bf_ref,
                win_ref, wpool_ref, wout_ref, zext_ref, pooled_ref, mixed_ref,
                stage_ref, pool_stage_ref, load_sem, pool_sem, *, seq_len):
    tm = x_ref.shape[0]
    a_width = gv_ref.shape[1]
    n_heads = a_width // A_HEAD_DIM
    group_dim = wpool_ref.shape[1]
    b_width = zext_ref.shape[1]
    u_col, v_col, z_col = 0, a_width, 2 * a_width
    step = pl.program_id(0)
    seq_pos0 = (step * tm) % seq_len
    next_seq_pos0 = ((step + 1) * tm) % seq_len

    def tile_body(first_step):
        if first_step:
            zext_ref[0:HALO, :] = jnp.zeros((HALO, zext_ref.shape[1]), _f32)
            pool_copy = pltpu.make_async_copy(wpool_hbm, pool_stage_ref, pool_sem.at[0])
            pool_copy.start()
            in_blocks = [(win_hbm, win_ref, c)
                         for c0, width in ((z_col, b_width), (v_col, a_width), (u_col, a_width))
                         for c in range(c0, c0 + width, LOAD_COLS)]
            out_blocks = [(wout_hbm, wout_ref, c) for c in range(0, wout_ref.shape[1], LOAD_COLS)]
            loader = _WeightLoader(in_blocks + out_blocks, stage_ref, load_sem)
            loader.start()
        z_blocks, a_blocks = b_width // LOAD_COLS, a_width // LOAD_COLS

        def weights_ready(n_blocks):
            if first_step:
                loader.need(n_blocks)

        def convert_mlp_weights(k, n=4):
            for src, dst in ((wup_ref, wup_bf_ref), (wdown_ref, wdown_bf_ref)):
                cols = src.shape[1] // n
                dst[:, k * cols:(k + 1) * cols] = src[:, k * cols:(k + 1) * cols].astype(_bf16)

        x = x_ref[...]
        hb = (x * _rms_scale(x) * gmix_ref[...]).astype(_bf16)

        def in_proj(c0, c1):
            return _dot(hb, win_ref[:, c0:c1])

        def pool_group(g):
            win = POOL_WINDOWS[g]
            gs = slice(g * group_dim, (g + 1) * group_dim)
            for r0 in range(0, tm, POOL_ROWS):
                zb = zext_ref[r0:r0 + HALO + POOL_ROWS, gs]
                s = zb
                shift = 1
                while shift < win:
                    s = s + pltpu.roll(s, shift, axis=0)
                    shift *= 2
                if r0 + 1 >= win:
                    mean = s[HALO:] * (1.0 / win)
                else:
                    pos = seq_pos0 + r0 + lax.broadcasted_iota(jnp.int32, (POOL_ROWS, 1), 0)
                    mean = s[HALO:] / jnp.minimum(pos + 1, win).astype(_f32)
                pooled_ref[r0:r0 + POOL_ROWS, gs] = (mean - zb[HALO:]).astype(_bf16)

        def pool_project(g):
            gs = slice(g * group_dim, (g + 1) * group_dim)
            y = _dot(pooled_ref[:, gs], wpool_ref[g]) * pscale_ref[:, gs]
            mixed_ref[:, a_width + g * group_dim:a_width + (g + 1) * group_dim] = y.astype(_bf16)

        row = lax.broadcasted_iota(jnp.int32, (CHUNK, CHUNK), 0)
        col = lax.broadcasted_iota(jnp.int32, (CHUNK, CHUNK), 1)
        causal = row >= col

        def normed_heads(vpart, hd0):
            out = []
            for k in range(vpart.shape[1] // A_HEAD_DIM):
                vh = vpart[:, k * A_HEAD_DIM:(k + 1) * A_HEAD_DIM]
                cs = slice((hd0 + k) * A_HEAD_DIM, (hd0 + k + 1) * A_HEAD_DIM)
                out.append((vh * _rms_scale(vh) * gv_ref[:, cs]).astype(_bf16))
            return out

        def gate_heads(upart, vns, hd0):
            for k, vn in enumerate(vns):
                hd = hd0 + k
                w = jnp.where(causal, ws_ref[hd], 0.0).astype(_bf16)
                bias = bst_ref[:, hd:hd + 1]
                n_chunks = tm // CHUNK
                vn_wide = jnp.concatenate([vn[c * CHUNK:(c + 1) * CHUNK] for c in range(n_chunks)], axis=1)
                mixed_wide = _dot(w, vn_wide) + bias
                for c in range(n_chunks):
                    rs = slice(c * CHUNK, (c + 1) * CHUNK)
                    uh = upart[rs, k * A_HEAD_DIM:(k + 1) * A_HEAD_DIM]
                    mixed_ref[rs, hd * A_HEAD_DIM:(hd + 1) * A_HEAD_DIM] = (
                        uh * mixed_wide[:, c * A_HEAD_DIM:(c + 1) * A_HEAD_DIM]).astype(_bf16)

        half_a, half_b, half_h, half_g = a_width // 2, zext_ref.shape[1] // 2, n_heads // 2, len(POOL_WINDOWS) // 2
        weights_ready(z_blocks)
        zext_ref[HALO:HALO + tm, 0:half_b] = in_proj(z_col, z_col + half_b)
        zext_ref[HALO:HALO + tm, half_b:] = in_proj(z_col + half_b, z_col + 2 * half_b)
        for g in range(half_g):
            pool_group(g)
        if first_step:
            pool_copy.wait()
            wpool_ref[...] = pool_stage_ref[...].astype(_bf16)
        for g in range(half_g):
            pool_project(g)
        convert_mlp_weights(0)
        weights_ready(z_blocks + a_blocks)
        v0 = _gelu_tanh(in_proj(v_col, v_col + half_a))
        convert_mlp_weights(1)
        for g in range(half_g, 2 * half_g):
            pool_group(g)
        zext_ref[0:HALO, :] = jnp.where(next_seq_pos0 == 0, 0.0, zext_ref[tm:tm + HALO, :])
        for g in range(half_g, 2 * half_g):
            pool_project(g)
        v1 = _gelu_tanh(in_proj(v_col + half_a, v_col + 2 * half_a))
        convert_mlp_weights(2)
        vn0 = normed_heads(v0, 0)
        weights_ready(z_blocks + 2 * a_blocks)
        u0 = _gelu_tanh(in_proj(u_col, u_col + half_a))
        convert_mlp_weights(3)
        vn1 = normed_heads(v1, half_h)
        gate_heads(u0, vn0, 0)
        u1 = _gelu_tanh(in_proj(u_col + half_a, u_col + 2 * half_a))
        gate_heads(u1, vn1, half_h)

        weights_ready(z_blocks + 2 * a_blocks + wout_ref.shape[1] // LOAD_COLS)
        o_ref[...] = x + _dot(mixed_ref[...], wout_ref[...])

    pl.when(step == 0)(functools.partial(tile_body, True))
    pl.when(step > 0)(functools.partial(tile_body, False))


def _ffn_kernel(x_ref, gffn_ref, wup_ref, wdown_ref, gfin_ref, o_ref, h_ref, *, final_norm):
    i, j = pl.program_id(0), pl.program_id(1)
    n_j = pl.num_programs(1)
    n_sub = wup_ref.shape[1] // FFN_SUB
    h_cur, h_next = h_ref.at[i % 2], h_ref.at[(i + 1) % 2]

    tm = x_ref.shape[0]

    def normalise_input(dst, r0=0, r1=tm):
        x = x_ref[r0:r1, :]
        dst[r0:r1, :] = (x * _rms_scale(x) * gffn_ref[...]).astype(_bf16)

    @pl.when((i == 0) & (j == 0))
    def _():
        normalise_input(h_cur)

    def step(first, last):
        def up(k):
            cols = slice(k * FFN_SUB, (k + 1) * FFN_SUB)
            return jnp.square(jnp.maximum(_dot(h_cur[...], wup_ref[:, cols]), 0.0)).astype(_bf16)

        def down(k, act):
            d = _dot(act, wdown_ref[k * FFN_SUB:(k + 1) * FFN_SUB, :])
            if first and k == 0:
                o_ref[...] = x_ref[...] + d
            elif last and k == n_sub - 1 and final_norm:
                y = o_ref[...] + d
                o_ref[...] = y * _rms_scale(y) * gfin_ref[...]
            else:
                o_ref[...] += d

        act = up(0)
        for k in range(1, n_sub):
            if last:
                rows = tm // (n_sub - 1) // BF16_SUBLANES * BF16_SUBLANES
                normalise_input(h_next, (k - 1) * rows, tm if k == n_sub - 1 else k * rows)
            nxt = up(k)
            down(k - 1, act)
            act = nxt
        down(n_sub - 1, act)

    pl.when(j == 0)(functools.partial(step, True, False))
    pl.when((j > 0) & (j < n_j - 1))(functools.partial(step, False, False))
    pl.when(j == n_j - 1)(functools.partial(step, False, True))


def _resident(shape):
    zeros = (0,) * len(shape)
    return pl.BlockSpec(shape, lambda *_: zeros, pipeline_mode=pl.Buffered(1))


def _row_slab(w, n_steps):
    rows = w.shape[0] // n_steps
    assert w.shape[0] % n_steps == 0 and rows % BF16_SUBLANES == 0
    return pl.BlockSpec((rows, w.shape[1]), lambda i: (i, 0))


def _mix_layer(xf, g_mix, w_in, g_v, w_s, b_s, w_pool, pool_scale, w_out, w_up, w_down, *, seq_len):
    m, d = xf.shape
    a_width = g_v.shape[0]
    b_width = pool_scale.shape[0]
    tm = MIX_TM
    n_steps = m // tm
    assert m % tm == 0 and seq_len % tm == 0 and tm % CHUNK == 0 and tm % POOL_ROWS == 0 and tm >= HALO
    for w in (w_in, w_out):
        assert w.shape[0] % LOAD_ROWS == 0 and w.shape[1] % LOAD_COLS == 0
    assert a_width % LOAD_COLS == 0 and b_width % LOAD_COLS == 0
    tile = pl.BlockSpec((tm, d), lambda i: (i, 0))
    hbm = pl.BlockSpec(memory_space=pl.ANY)
    small = (
        g_mix.reshape(1, d),
        g_v.reshape(1, a_width),
        w_s,
        b_s.T,
        pool_scale.reshape(1, b_width),
    )
    slabs = [_row_slab(w_up, n_steps), _row_slab(w_down, n_steps)]
    return pl.pallas_call(
        functools.partial(_mix_kernel, seq_len=seq_len),
        grid=(n_steps,),
        in_specs=[tile] + [_resident(op.shape) for op in small] + [hbm, hbm, hbm] + slabs,
        out_specs=[tile] + slabs,
        out_shape=[jax.ShapeDtypeStruct((m, d), _f32),
                   jax.ShapeDtypeStruct(w_up.shape, _bf16),
                   jax.ShapeDtypeStruct(w_down.shape, _bf16)],
        scratch_shapes=[
            pltpu.VMEM(w_in.shape, _bf16),
            pltpu.VMEM(w_pool.shape, _bf16),
            pltpu.VMEM(w_out.shape, _bf16),
            pltpu.VMEM((HALO + tm, b_width), _f32),
            pltpu.VMEM((tm, b_width), _bf16),
            pltpu.VMEM((tm, a_width + b_width), _bf16),
            pltpu.VMEM((LOAD_SLOTS, LOAD_ROWS, LOAD_COLS), _f32),
            pltpu.VMEM(w_pool.shape, _f32),
            pltpu.SemaphoreType.DMA((LOAD_SLOTS,)),
            pltpu.SemaphoreType.DMA((1,)),
        ],
        compiler_params=pltpu.CompilerParams(
            dimension_semantics=("arbitrary",),
            vmem_limit_bytes=MIX_VMEM_LIMIT_BYTES),
        name="mix_layer",
    )(xf, *small, w_in, w_pool, w_out, w_up, w_down)


def _ffn_layer(xf, g_ffn, w_up_bf, w_down_bf, g_final, *, final_norm):
    m, d = xf.shape
    d_ff = w_up_bf.shape[1]
    tm, tf = FFN_TM, FFN_TF
    n_tiles, n_j = m // tm, d_ff // tf
    assert m % tm == 0 and d_ff % tf == 0 and tf % FFN_SUB == 0 and n_j >= 2
    tile = pl.BlockSpec((tm, d), lambda i, j: (i, 0))
    tile_ahead = pl.BlockSpec((tm, d), lambda i, j: (jnp.minimum(i + (j + 1) // n_j, n_tiles - 1), 0))
    return pl.pallas_call(
        functools.partial(_ffn_kernel, final_norm=final_norm),
        grid=(n_tiles, n_j),
        in_specs=[
            tile_ahead,
            _resident((1, d)),
            pl.BlockSpec((d, tf), lambda i, j: (0, j)),
            pl.BlockSpec((tf, d), lambda i, j: (j, 0)),
            _resident((1, d)),
        ],
        out_specs=tile,
        out_shape=jax.ShapeDtypeStruct((m, d), _f32),
        scratch_shapes=[
            pltpu.VMEM((2, tm, d), _bf16),
        ],
        compiler_params=pltpu.CompilerParams(
            dimension_semantics=("arbitrary", "arbitrary"),
            vmem_limit_bytes=VMEM_LIMIT_BYTES),
        name="ffn_layer",
    )(xf, g_ffn.reshape(1, d), w_up_bf, w_down_bf, g_final.reshape(1, d))


def kernel(x, g_mix, w_in, g_v, w_s, b_s, w_pool, pool_scale, w_out, g_ffn, w_up, w_down, g_final):
    bsz, seq_len, d = x.shape
    depth = g_mix.shape[0]
    xf = x.reshape(bsz * seq_len, d)
    for layer in range(depth):
        xf, w_up_bf, w_down_bf = _mix_layer(
            xf, g_mix[layer], w_in[layer], g_v[layer], w_s[layer], b_s[layer], w_pool[layer],
            pool_scale[layer], w_out[layer], w_up[layer], w_down[layer], seq_len=seq_len)
        xf = _ffn_layer(xf, g_ffn[layer], w_up_bf, w_down_bf, g_final, final_norm=(layer == depth - 1))
    return xf.reshape(bsz, seq_len, d)
```

```python
import functools

import jax
import jax.numpy as jnp
from jax import lax
from jax.experimental import pallas as pl
from jax.experimental.pallas import tpu as pltpu

CHUNK = 128
A_HEAD_DIM = 128
POOL_WINDOWS = (2, 4, 8, 16)
EPS = 1e-6

HALO = 16
POOL_ROWS = 64
BF16_SUBLANES = 16
MIX_TM = 256
FFN_TM = 512
FFN_TF = 2048
FFN_SUB = 512
VMEM_LIMIT_BYTES = 60 * 1024 * 1024
MIX_VMEM_LIMIT_BYTES = 63 * 1024 * 1024

LOAD_ROWS = 256
LOAD_COLS = 1024
LOAD_SLOTS = 12

_bf16 = jnp.bfloat16
_f32 = jnp.float32


def _rms_scale(x):
    return lax.rsqrt(jnp.mean(x * x, axis=-1, keepdims=True) + EPS)


def _gelu_tanh(x):
    c = 0.7978845608028654
    t = jnp.tanh(x * ((x * x) * (c * 0.044715) + c))
    return x * (0.5 * t + 0.5)


def _dot(a, b):
    return jnp.dot(a, b, preferred_element_type=_f32)


class _WeightLoader:
    def __init__(self, col_blocks, stage, sem):
        self.pieces = [(src, dst, r, c) for src, dst, c in col_blocks for r in range(0, src.shape[0], LOAD_ROWS)]
        self.block_end = []
        for src, _, _ in col_blocks:
            self.block_end.append((self.block_end[-1] if self.block_end else 0) + src.shape[0] // LOAD_ROWS)
        self.stage, self.sem = stage, sem
        self.done = 0

    def _read(self, i):
        src, _, r, c = self.pieces[i]
        slot = i % LOAD_SLOTS
        return pltpu.make_async_copy(src.at[r:r + LOAD_ROWS, c:c + LOAD_COLS], self.stage.at[slot], self.sem.at[slot])

    def start(self):
        for i in range(min(LOAD_SLOTS - 1, len(self.pieces))):
            self._read(i).start()

    def need(self, n_blocks):
        while self.done < self.block_end[n_blocks - 1]:
            i = self.done
            if i + LOAD_SLOTS - 1 < len(self.pieces):
                self._read(i + LOAD_SLOTS - 1).start()
            self._read(i).wait()
            _, dst, r, c = self.pieces[i]
            dst[r:r + LOAD_ROWS, c:c + LOAD_COLS] = self.stage[i % LOAD_SLOTS].astype(_bf16)
            self.done += 1


def _mix_kernel(x_ref, gmix_ref, gv_ref, ws_ref, bst_ref, pscale_ref,
                win_hbm, wpool_hbm, wout_hbm, wup_ref, wdown_ref,
                o_ref, wup_bf_ref, wdown_bf_ref,
                win_ref, wpool_ref, wout_ref, wsm_ref, zext_ref, pooled_ref, mixed_ref,
                stage_ref, pool_stage_ref, load_sem, pool_sem, *, seq_len):
    tm = x_ref.shape[0]
    a_width = gv_ref.shape[1]
    n_heads = a_width // A_HEAD_DIM
    group_dim = wpool_ref.shape[1]
    b_width = zext_ref.shape[1]
    u_col, v_col, z_col = 0, a_width, 2 * a_width
    step = pl.program_id(0)
    seq_pos0 = (step * tm) % seq_len
    next_seq_pos0 = ((step + 1) * tm) % seq_len

    def tile_body(first_step):
        if first_step:
            zext_ref[0:HALO, :] = jnp.zeros((HALO, zext_ref.shape[1]), _f32)
            pool_copy = pltpu.make_async_copy(wpool_hbm, pool_stage_ref, pool_sem.at[0])
            pool_copy.start()
            in_blocks = [(win_hbm, win_ref, c)
                         for c0, width in ((z_col, b_width), (v_col, a_width), (u_col, a_width))
                         for c in range(c0, c0 + width, LOAD_COLS)]
            out_blocks = [(wout_hbm, wout_ref, c) for c in range(0, wout_ref.shape[1], LOAD_COLS)]
            loader = _WeightLoader(in_blocks + out_blocks, stage_ref, load_sem)
            loader.start()
        z_blocks, a_blocks = b_width // LOAD_COLS, a_width // LOAD_COLS

        def weights_ready(n_blocks):
            if first_step:
                loader.need(n_blocks)

        def convert_mlp_weights(k, n=4):
            for src, dst in ((wup_ref, wup_bf_ref), (wdown_ref, wdown_bf_ref)):
                cols = src.shape[1] // n
                dst[:, k * cols:(k + 1) * cols] = src[:, k * cols:(k + 1) * cols].astype(_bf16)

        x = x_ref[...]
        hb = (x * _rms_scale(x) * gmix_ref[...]).astype(_bf16)

        def in_proj(c0, c1):
            return _dot(hb, win_ref[:, c0:c1])

        def pool_group(g):
            win = POOL_WINDOWS[g]
            gs = slice(g * group_dim, (g + 1) * group_dim)
            for r0 in range(0, tm, POOL_ROWS):
                zb = zext_ref[r0:r0 + HALO + POOL_ROWS, gs]
                s = zb
                shift = 1
                while shift < win:
                    s = s + pltpu.roll(s, shift, axis=0)
                    shift *= 2
                if r0 + 1 >= win:
                    mean = s[HALO:] * (1.0 / win)
                else:
                    pos = seq_pos0 + r0 + lax.broadcasted_iota(jnp.int32, (POOL_ROWS, 1), 0)
                    mean = s[HALO:] / jnp.minimum(pos + 1, win).astype(_f32)
                pooled_ref[r0:r0 + POOL_ROWS, gs] = (mean - zb[HALO:]).astype(_bf16)

        def pool_project(g):
            gs = slice(g * group_dim, (g + 1) * group_dim)
            y = _dot(pooled_ref[:, gs], wpool_ref[g]) * pscale_ref[:, gs]
            mixed_ref[:, a_width + g * group_dim:a_width + (g + 1) * group_dim] = y.astype(_bf16)

        if first_step:
            row = lax.broadcasted_iota(jnp.int32, (CHUNK, CHUNK), 0)
            col = lax.broadcasted_iota(jnp.int32, (CHUNK, CHUNK), 1)
            for hd in range(n_heads):
                wsm_ref[hd] = jnp.where(row >= col, ws_ref[hd], 0.0).astype(_bf16)

        def normed_heads(vpart, hd0):
            out = []
            for k in range(vpart.shape[1] // A_HEAD_DIM):
                vh = vpart[:, k * A_HEAD_DIM:(k + 1) * A_HEAD_DIM]
                cs = slice((hd0 + k) * A_HEAD_DIM, (hd0 + k + 1) * A_HEAD_DIM)
                out.append((vh * _rms_scale(vh) * gv_ref[:, cs]).astype(_bf16))
            return out

        def gate_heads(upart, vns, hd0):
            for k, vn in enumerate(vns):
                hd = hd0 + k
                w = wsm_ref[hd]
                bias = bst_ref[:, hd:hd + 1]
                n_chunks = tm // CHUNK
                vn_wide = jnp.concatenate([vn[c * CHUNK:(c + 1) * CHUNK] for c in range(n_chunks)], axis=1)
                mixed_wide = _dot(w, vn_wide) + bias
                for c in range(n_chunks):
                    rs = slice(c * CHUNK, (c + 1) * CHUNK)
                    uh = upart[rs, k * A_HEAD_DIM:(k + 1) * A_HEAD_DIM]
                    mixed_ref[rs, hd * A_HEAD_DIM:(hd + 1) * A_HEAD_DIM] = (
                        uh * mixed_wide[:, c * A_HEAD_DIM:(c + 1) * A_HEAD_DIM]).astype(_bf16)

        half_a, half_b, half_h, half_g = a_width // 2, zext_ref.shape[1] // 2, n_heads // 2, len(POOL_WINDOWS) // 2
        weights_ready(z_blocks)
        zext_ref[HALO:HALO + tm, 0:half_b] = in_proj(z_col, z_col + half_b)
        zext_ref[HALO:HALO + tm, half_b:] = in_proj(z_col + half_b, z_col + 2 * half_b)
        for g in range(half_g):
            pool_group(g)
        if first_step:
            pool_copy.wait()
            wpool_ref[...] = pool_stage_ref[...].astype(_bf16)
        for g in range(half_g):
            pool_project(g)
        convert_mlp_weights(0)
        weights_ready(z_blocks + a_blocks)
        v0 = _gelu_tanh(in_proj(v_col, v_col + half_a))
        convert_mlp_weights(1)
        for g in range(half_g, 2 * half_g):
            pool_group(g)
        zext_ref[0:HALO, :] = jnp.where(next_seq_pos0 == 0, 0.0, zext_ref[tm:tm + HALO, :])
        for g in range(half_g, 2 * half_g):
            pool_project(g)
        v1 = _gelu_tanh(in_proj(v_col + half_a, v_col + 2 * half_a))
        convert_mlp_weights(2)
        vn0 = normed_heads(v0, 0)
        weights_ready(z_blocks + 2 * a_blocks)
        u0 = _gelu_tanh(in_proj(u_col, u_col + half_a))
        convert_mlp_weights(3)
        vn1 = normed_heads(v1, half_h)
        gate_heads(u0, vn0, 0)
        u1 = _gelu_tanh(in_proj(u_col + half_a, u_col + 2 * half_a))
        gate_heads(u1, vn1, half_h)

        weights_ready(z_blocks + 2 * a_blocks + wout_ref.shape[1] // LOAD_COLS)
        o_ref[...] = x + _dot(mixed_ref[...], wout_ref[...])

    pl.when(step == 0)(functools.partial(tile_body, True))
    pl.when(step > 0)(functools.partial(tile_body, False))


def _ffn_kernel(x_ref, gffn_ref, wup_ref, wdown_ref, gfin_ref, o_ref, h_ref, *, final_norm):
    i, j = pl.program_id(0), pl.program_id(1)
    n_j = pl.num_programs(1)
    n_sub = wup_ref.shape[1] // FFN_SUB
    h_cur, h_next = h_ref.at[i % 2], h_ref.at[(i + 1) % 2]

    tm = x_ref.shape[0]

    def normalise_input(dst, r0=0, r1=tm):
        x = x_ref[r0:r1, :]
        dst[r0:r1, :] = (x * _rms_scale(x) * gffn_ref[...]).astype(_bf16)

    @pl.when((i == 0) & (j == 0))
    def _():
        normalise_input(h_cur)

    def step(first, last):
        def up(k):
            cols = slice(k * FFN_SUB, (k + 1) * FFN_SUB)
            return jnp.square(jnp.maximum(_dot(h_cur[...], wup_ref[:, cols]), 0.0)).astype(_bf16)

        def down(k, act):
            d = _dot(act, wdown_ref[k * FFN_SUB:(k + 1) * FFN_SUB, :])
            if first and k == 0:
                o_ref[...] = x_ref[...] + d
            elif last and k == n_sub - 1 and final_norm:
                y = o_ref[...] + d
                o_ref[...] = y * _rms_scale(y) * gfin_ref[...]
            else:
                o_ref[...] += d

        act = up(0)
        for k in range(1, n_sub):
            if last:
                rows = tm // (n_sub - 1) // BF16_SUBLANES * BF16_SUBLANES
                normalise_input(h_next, (k - 1) * rows, tm if k == n_sub - 1 else k * rows)
            nxt = up(k)
            down(k - 1, act)
            act = nxt
        down(n_sub - 1, act)

    pl.when(j == 0)(functools.partial(step, True, False))
    pl.when((j > 0) & (j < n_j - 1))(functools.partial(step, False, False))
    pl.when(j == n_j - 1)(functools.partial(step, False, True))


def _resident(shape):
    zeros = (0,) * len(shape)
    return pl.BlockSpec(shape, lambda *_: zeros, pipeline_mode=pl.Buffered(1))


def _row_slab(w, n_steps):
    rows = w.shape[0] // n_steps
    assert w.shape[0] % n_steps == 0 and rows % BF16_SUBLANES == 0
    return pl.BlockSpec((rows, w.shape[1]), lambda i: (i, 0))


def _mix_layer(xf, g_mix, w_in, g_v, w_s, b_s, w_pool, pool_scale, w_out, w_up, w_down, *, seq_len):
    m, d = xf.shape
    a_width = g_v.shape[0]
    b_width = pool_scale.shape[0]
    tm = MIX_TM
    n_steps = m // tm
    assert m % tm == 0 and seq_len % tm == 0 and tm % CHUNK == 0 and tm % POOL_ROWS == 0 and tm >= HALO
    for w in (w_in, w_out):
        assert w.shape[0] % LOAD_ROWS == 0 and w.shape[1] % LOAD_COLS == 0
    assert a_width % LOAD_COLS == 0 and b_width % LOAD_COLS == 0
    tile = pl.BlockSpec((tm, d), lambda i: (i, 0))
    hbm = pl.BlockSpec(memory_space=pl.ANY)
    small = (
        g_mix.reshape(1, d),
        g_v.reshape(1, a_width),
        w_s,
        b_s.T,
        pool_scale.reshape(1, b_width),
    )
    slabs = [_row_slab(w_up, n_steps), _row_slab(w_down, n_steps)]
    return pl.pallas_call(
        functools.partial(_mix_kernel, seq_len=seq_len),
        grid=(n_steps,),
        in_specs=[tile] + [_resident(op.shape) for op in small] + [hbm, hbm, hbm] + slabs,
        out_specs=[tile] + slabs,
        out_shape=[jax.ShapeDtypeStruct((m, d), _f32),
                   jax.ShapeDtypeStruct(w_up.shape, _bf16),
                   jax.ShapeDtypeStruct(w_down.shape, _bf16)],
        scratch_shapes=[
            pltpu.VMEM(w_in.shape, _bf16),
            pltpu.VMEM(w_pool.shape, _bf16),
            pltpu.VMEM(w_out.shape, _bf16),
            pltpu.VMEM(w_s.shape, _bf16),
            pltpu.VMEM((HALO + tm, b_width), _f32),
            pltpu.VMEM((tm, b_width), _bf16),
            pltpu.VMEM((tm, a_width + b_width), _bf16),
            pltpu.VMEM((LOAD_SLOTS, LOAD_ROWS, LOAD_COLS), _f32),
            pltpu.VMEM(w_pool.shape, _f32),
            pltpu.SemaphoreType.DMA((LOAD_SLOTS,)),
            pltpu.SemaphoreType.DMA((1,)),
        ],
        compiler_params=pltpu.CompilerParams(
            dimension_semantics=("arbitrary",),
            vmem_limit_bytes=MIX_VMEM_LIMIT_BYTES),
        name="mix_layer",
    )(xf, *small, w_in, w_pool, w_out, w_up, w_down)


def _ffn_layer(xf, g_ffn, w_up_bf, w_down_bf, g_final, *, final_norm):
    m, d = xf.shape
    d_ff = w_up_bf.shape[1]
    tm, tf = FFN_TM, FFN_TF
    n_tiles, n_j = m // tm, d_ff // tf
    assert m % tm == 0 and d_ff % tf == 0 and tf % FFN_SUB == 0 and n_j >= 2
    tile = pl.BlockSpec((tm, d), lambda i, j: (i, 0))
    tile_ahead = pl.BlockSpec((tm, d), lambda i, j: (jnp.minimum(i + (j + 1) // n_j, n_tiles - 1), 0))
    return pl.pallas_call(
        functools.partial(_ffn_kernel, final_norm=final_norm),
        grid=(n_tiles, n_j),
        in_specs=[
            tile_ahead,
            _resident((1, d)),
            pl.BlockSpec((d, tf), lambda i, j: (0, j)),
            pl.BlockSpec((tf, d), lambda i, j: (j, 0)),
            _resident((1, d)),
        ],
        out_specs=tile,
        out_shape=jax.ShapeDtypeStruct((m, d), _f32),
        scratch_shapes=[
            pltpu.VMEM((2, tm, d), _bf16),
        ],
        compiler_params=pltpu.CompilerParams(
            dimension_semantics=("arbitrary", "arbitrary"),
            vmem_limit_bytes=VMEM_LIMIT_BYTES),
        name="ffn_layer",
    )(xf, g_ffn.reshape(1, d), w_up_bf, w_down_bf, g_final.reshape(1, d))


def kernel(x, g_mix, w_in, g_v, w_s, b_s, w_pool, pool_scale, w_out, g_ffn, w_up, w_down, g_final):
    bsz, seq_len, d = x.shape
    depth = g_mix.shape[0]
    xf = x.reshape(bsz * seq_len, d)
    for layer in range(depth):
        xf, w_up_bf, w_down_bf = _mix_layer(
            xf, g_mix[layer], w_in[layer], g_v[layer], w_s[layer], b_s[layer], w_pool[layer],
            pool_scale[layer], w_out[layer], w_up[layer], w_down[layer], seq_len=seq_len)
        xf = _ffn_layer(xf, g_ffn[layer], w_up_bf, w_down_bf, g_final, final_norm=(layer == depth - 1))
    return xf.reshape(bsz, seq_len, d)
```
